```python
import jax, jax.numpy as jnp
from jax import lax
import numpy as np

D_MODEL = 2048
BATCH = 8
SEQ = 2048
DEPTH = 1

CHUNK = 64
Q_BLOCK = 128
EPS = 1e-6

MLA_HEADS = 8
Q_LORA = 768
KV_LORA = 512
QK_NOPE = 128
QK_ROPE = 64
V_HEAD = 128
ROPE_THETA = 10000.0
MLA_WIDTH = 1024

SSM_HEADS = 16
SSM_HEAD_DIM = 64
SSM_INNER = 1024
SSM_GROUPS = 2
SSM_STATE = 128
SSM_CONV = 4
SSM_CONV_CH = 1536

MIX_WIDTH = 2048
IN_COLS = 3920

D_FF = 5632
FFN_CONV = 3

kernel_name = "hymba_mla_ssd_convffn_sandwich"


def rms_norm(x, g):
    xf = x.astype(jnp.float32)
    y = xf * lax.rsqrt(jnp.mean(xf * xf, axis=-1, keepdims=True) + EPS)
    return (y * g.astype(jnp.float32)).astype(x.dtype)


def causal_dwconv(x, w, b):
    K, C = w.shape
    y = lax.conv_general_dilated(
        x, w[:, None, :].astype(x.dtype), window_strides=(1,), padding=[(K - 1, 0)],
        dimension_numbers=('NWC', 'WIO', 'NWC'), feature_group_count=C)
    return y + b.astype(x.dtype)


def rope_tables(S):
    inv = 1.0 / (ROPE_THETA ** (jnp.arange(0, QK_ROPE, 2, dtype=jnp.float32) / QK_ROPE))
    ang = jnp.arange(S, dtype=jnp.float32)[:, None] * inv[None, :]
    return jnp.cos(ang), jnp.sin(ang)


def rotate(x, cos, sin):
    x1, x2 = jnp.split(x, 2, axis=-1)
    c = cos.astype(x.dtype)
    s = sin.astype(x.dtype)
    return jnp.concatenate([x1 * c - x2 * s, x1 * s + x2 * c], axis=-1)


def mla_mixer(c_q, c_kv, k_rope, q_norm_g, w_uq, kv_norm_g, w_ukv):
    Bsz, S, _ = c_q.shape
    q = (rms_norm(c_q, q_norm_g) @ w_uq).reshape(Bsz, S, MLA_HEADS, QK_NOPE + QK_ROPE)
    q_nope, q_rope = q[..., :QK_NOPE], q[..., QK_NOPE:]
    kv = (rms_norm(c_kv, kv_norm_g) @ w_ukv).reshape(Bsz, S, MLA_HEADS, QK_NOPE + V_HEAD)
    k_nope, v = kv[..., :QK_NOPE], kv[..., QK_NOPE:]
    cos, sin = rope_tables(S)
    q_rope = rotate(q_rope, cos[None, :, None, :], sin[None, :, None, :])
    k_rope = rotate(k_rope, cos[None], sin[None])
    scale = (QK_NOPE + QK_ROPE) ** -0.5
    nblk = S // Q_BLOCK
    key_chunk = jnp.arange(S) // CHUNK
    qn_b = q_nope.reshape(Bsz, nblk, Q_BLOCK, MLA_HEADS, QK_NOPE).transpose(1, 0, 2, 3, 4)
    qr_b = q_rope.reshape(Bsz, nblk, Q_BLOCK, MLA_HEADS, QK_ROPE).transpose(1, 0, 2, 3, 4)

    def block(args):
        qn, qr, i = args
        s = (jnp.einsum('bqhd,bkhd->bhqk', qn, k_nope)
             + jnp.einsum('bqhr,bkr->bhqk', qr, k_rope)).astype(jnp.float32) * scale
        q_chunk = (i * Q_BLOCK + jnp.arange(Q_BLOCK)) // CHUNK
        mask = key_chunk[None, :] <= q_chunk[:, None]
        s = jnp.where(mask, s, -1e30)
        p = jax.nn.softmax(s, axis=-1).astype(v.dtype)
        return jnp.einsum('bhqk,bkhd->bqhd', p, v)

    o = lax.map(block, (qn_b, qr_b, jnp.arange(nblk)))
    return o.transpose(1, 0, 2, 3, 4).reshape(Bsz, S, MLA_WIDTH)


def segsum(a):
    T = a.shape[-1]
    cs = jnp.cumsum(a, axis=-1)
    d = cs[..., :, None] - cs[..., None, :]
    return jnp.where(jnp.tril(jnp.ones((T, T), dtype=bool)), d, -jnp.inf)


def ssd_chunked(x, a, b, c):
    Bsz, S, H, P = x.shape
    N = b.shape[-1]
    nc = S // CHUNK
    x = x.reshape(Bsz, nc, CHUNK, H, P)
    b = b.reshape(Bsz, nc, CHUNK, H, N)
    c = c.reshape(Bsz, nc, CHUNK, H, N)
    a = a.reshape(Bsz, nc, CHUNK, H).transpose(0, 3, 1, 2)
    a_cs = jnp.cumsum(a, axis=-1)
    L = jnp.exp(segsum(a))
    y_diag = jnp.einsum('bclhn,bcshn,bhcls,bcshp->bclhp', c, b, L, x)
    decay_states = jnp.exp(a_cs[..., -1:] - a_cs)
    states = jnp.einsum('bclhn,bhcl,bclhp->bchpn', b, decay_states, x)
    chunk_decay = jnp.exp(a_cs[..., -1])

    def step(h, inp):
        st, dec = inp
        return h * dec[..., None, None] + st, h

    h0 = jnp.zeros((Bsz, H, P, N), x.dtype)
    _, states_in = lax.scan(step, h0, (states.transpose(1, 0, 2, 3, 4), chunk_decay.transpose(2, 0, 1)))
    states_in = states_in.transpose(1, 0, 2, 3, 4)
    y_off = jnp.einsum('bclhn,bchpn,bhcl->bclhp', c, states_in, jnp.exp(a_cs))
    return (y_diag + y_off).reshape(Bsz, S, H, P)


def ssd_mixer(z, xbc, dt, conv_w, conv_b, dt_bias, a_log, d_skip, norm_g):
    Bsz, S, _ = z.shape
    f32 = jnp.float32
    xbc = jax.nn.silu(causal_dwconv(xbc, conv_w, conv_b))
    gn = SSM_GROUPS * SSM_STATE
    xs = xbc[..., :SSM_INNER].reshape(Bsz, S, SSM_HEADS, SSM_HEAD_DIM).astype(f32)
    bm = xbc[..., SSM_INNER:SSM_INNER + gn].reshape(Bsz, S, SSM_GROUPS, SSM_STATE)
    cm = xbc[..., SSM_INNER + gn:].reshape(Bsz, S, SSM_GROUPS, SSM_STATE)
    rep = SSM_HEADS // SSM_GROUPS
    bh = jnp.repeat(bm, rep, axis=2).astype(f32)
    ch = jnp.repeat(cm, rep, axis=2).astype(f32)
    dtf = jax.nn.softplus(dt.astype(f32) + dt_bias.astype(f32))
    A = -jnp.exp(a_log.astype(f32))
    y = ssd_chunked(xs * dtf[..., None], dtf * A, bh, ch)
    y = y + xs * d_skip.astype(f32)[:, None]
    y = y.reshape(Bsz, S, SSM_INNER) * jax.nn.silu(z.astype(f32))
    yg = y.reshape(Bsz, S, SSM_GROUPS, SSM_INNER // SSM_GROUPS)
    yg = yg * lax.rsqrt(jnp.mean(yg * yg, axis=-1, keepdims=True) + EPS)
    y = yg.reshape(Bsz, S, SSM_INNER) * norm_g.astype(f32)
    return y.astype(z.dtype)


def conv_glu_ffn(h, w_gate, w_up, conv_w, conv_b, w_down):
    g = causal_dwconv(h @ w_gate, conv_w, conv_b)
    return (jax.nn.gelu(g, approximate=True) * (h @ w_up)) @ w_down


def hybrid_layer(x, mix_pre_g, w_in, q_norm_g, w_uq, kv_norm_g, w_ukv, ssm_conv_w, ssm_conv_b,
                 dt_bias, a_log, d_skip, ssm_norm_g, w_out, mix_post_g, ffn_pre_g, w_gate, w_up,
                 ffn_conv_w, ffn_conv_b, w_down, ffn_post_g):
    u = rms_norm(x, mix_pre_g) @ w_in
    cuts = np.cumsum([Q_LORA, KV_LORA, QK_ROPE, SSM_INNER, SSM_CONV_CH]).tolist()
    c_q, c_kv, k_rope, z, xbc, dt = jnp.split(u, cuts, axis=-1)
    a_out = mla_mixer(c_q, c_kv, k_rope, q_norm_g, w_uq, kv_norm_g, w_ukv)
    b_out = ssd_mixer(z, xbc, dt, ssm_conv_w, ssm_conv_b, dt_bias, a_log, d_skip, ssm_norm_g)
    mix = jnp.concatenate([a_out, b_out], axis=-1) @ w_out
    x = x + rms_norm(mix, mix_post_g)
    f = conv_glu_ffn(rms_norm(x, ffn_pre_g), w_gate, w_up, ffn_conv_w, ffn_conv_b, w_down)
    return x + rms_norm(f, ffn_post_g)


def setup_inputs(seed: int = 0) -> dict:
    key = jax.random.key(seed)
    ks = jax.random.split(key, 24)
    f32 = jnp.float32

    def nrm(k, shape, fan_in):
        return jax.random.normal(k, shape, f32) * (fan_in ** -0.5)

    def gain(k, n):
        return 1.0 + 0.02 * jax.random.normal(k, (DEPTH, n), f32)

    dt0 = jnp.exp(jax.random.uniform(ks[9], (DEPTH, SSM_HEADS), f32, np.log(1e-3), np.log(1e-1)))
    return {
        "x": jax.random.normal(ks[0], (BATCH, SEQ, D_MODEL), f32),
        "mix_pre_g": gain(ks[1], D_MODEL),
        "w_in": nrm(ks[2], (DEPTH, D_MODEL, IN_COLS), D_MODEL),
        "q_norm_g": gain(ks[3], Q_LORA),
        "w_uq": nrm(ks[4], (DEPTH, Q_LORA, MLA_HEADS * (QK_NOPE + QK_ROPE)), Q_LORA),
        "kv_norm_g": gain(ks[5], KV_LORA),
        "w_ukv": nrm(ks[6], (DEPTH, KV_LORA, MLA_HEADS * (QK_NOPE + V_HEAD)), KV_LORA),
        "ssm_conv_w": nrm(ks[7], (DEPTH, SSM_CONV, SSM_CONV_CH), SSM_CONV),
        "ssm_conv_b": 0.01 * jax.random.normal(ks[8], (DEPTH, SSM_CONV_CH), f32),
        "dt_bias": dt0 + jnp.log(-jnp.expm1(-dt0)),
        "a_log": jnp.log(jax.random.uniform(ks[10], (DEPTH, SSM_HEADS), f32, 1.0, 16.0)),
        "d_skip": gain(ks[11], SSM_HEADS),
        "ssm_norm_g": gain(ks[12], SSM_INNER),
        "w_out": nrm(ks[13], (DEPTH, MIX_WIDTH, D_MODEL), MIX_WIDTH),
        "mix_post_g": gain(ks[14], D_MODEL),
        "ffn_pre_g": gain(ks[15], D_MODEL),
        "w_gate": nrm(ks[16], (DEPTH, D_MODEL, D_FF), D_MODEL),
        "w_up": nrm(ks[17], (DEPTH, D_MODEL, D_FF), D_MODEL),
        "ffn_conv_w": nrm(ks[18], (DEPTH, FFN_CONV, D_FF), FFN_CONV),
        "ffn_conv_b": 0.01 * jax.random.normal(ks[19], (DEPTH, D_FF), f32),
        "w_down": nrm(ks[20], (DEPTH, D_FF, D_MODEL), D_FF),
        "ffn_post_g": gain(ks[21], D_MODEL),
    }


def reference(x, mix_pre_g, w_in, q_norm_g, w_uq, kv_norm_g, w_ukv, ssm_conv_w, ssm_conv_b,
              dt_bias, a_log, d_skip, ssm_norm_g, w_out, mix_post_g, ffn_pre_g, w_gate, w_up,
              ffn_conv_w, ffn_conv_b, w_down, ffn_post_g):
    for l in range(DEPTH):
        x = hybrid_layer(x, mix_pre_g[l], w_in[l], q_norm_g[l], w_uq[l], kv_norm_g[l], w_ukv[l],
                         ssm_conv_w[l], ssm_conv_b[l], dt_bias[l], a_log[l], d_skip[l], ssm_norm_g[l],
                         w_out[l], mix_post_g[l], ffn_pre_g[l], w_gate[l], w_up[l],
                         ffn_conv_w[l], ffn_conv_b[l], w_down[l], ffn_post_g[l])
    return x
```

```python
import functools

import jax
import jax.numpy as jnp
import numpy as np
from jax import lax
from jax.experimental import pallas as pl
from jax.experimental.pallas import tpu as pltpu

F32 = jnp.float32
BF16 = jnp.bfloat16

D_MODEL = 2048
CHUNK = 64
EPS = 1e-6

MLA_HEADS = 8
Q_LORA = 768
KV_LORA = 512
QK_NOPE = 128
QK_ROPE = 64
V_HEAD = 128
ROPE_THETA = 10000.0
MLA_WIDTH = MLA_HEADS * V_HEAD
Q_HEAD_PAD = 256

SSM_HEADS = 16
SSM_HEAD_DIM = 64
SSM_INNER = SSM_HEADS * SSM_HEAD_DIM
SSM_GROUPS = 2
SSM_STATE = 128
SSM_CONV = 4
SSM_CONV_CH = SSM_INNER + 2 * SSM_GROUPS * SSM_STATE
GROUP_WIDTH = SSM_INNER // SSM_GROUPS

D_FF = 5632
FFN_CONV = 3

LANE = 128
U_COLS = 4096
U_CQ = 0
U_KR = Q_LORA
U_DT = U_KR + LANE
U_CKV = U_DT + LANE
U_XBC = U_CKV + KV_LORA
U_Z = U_XBC + SSM_CONV_CH

VMEM_LIMIT = 56 * 1024 * 1024


def _rms(x, g):
    return x * lax.rsqrt(jnp.mean(x * x, axis=-1, keepdims=True) + EPS) * g


def _sigmoid(x):
    return 1.0 / (1.0 + jnp.exp(-x))


def _dot(a, b):
    return jnp.dot(a, b, preferred_element_type=F32)


def _dot_nt(a, b):
    return lax.dot_general(a, b, (((1,), (1,)), ((), ())), preferred_element_type=F32)


def _dot_tn(a, b):
    return lax.dot_general(a, b, (((0,), (0,)), ((), ())), preferred_element_type=F32)


def _dot_exact(a, b):
    return jnp.dot(a, b, preferred_element_type=F32, precision=lax.Precision.HIGHEST)


def _in_proj_kernel(x_ref, g_ref, w_ref, o_ref, h_ref):
    @pl.when(pl.program_id(1) == 0)
    def _():
        h_ref[...] = _rms(x_ref[...], g_ref[...]).astype(BF16)

    o_ref[...] = _dot(h_ref[...], w_ref[...])


def _in_proj(x2, g, w, tm=512, tn=1024):
    t = x2.shape[0]
    return pl.pallas_call(
        _in_proj_kernel,
        grid=(t // tm, U_COLS // tn),
        in_specs=[
            pl.BlockSpec((tm, D_MODEL), lambda i, j: (i, 0)),
            pl.BlockSpec((1, D_MODEL), lambda i, j: (0, 0)),
            pl.BlockSpec((D_MODEL, tn), lambda i, j: (0, j)),
        ],
        out_specs=pl.BlockSpec((tm, tn), lambda i, j: (i, j)),
        out_shape=jax.ShapeDtypeStruct((t, U_COLS), F32),
        scratch_shapes=[pltpu.VMEM((tm, D_MODEL), BF16)],
        compiler_params=pltpu.CompilerParams(
            dimension_semantics=("parallel", "arbitrary"), vmem_limit_bytes=VMEM_LIMIT),
        name="in_proj",
    )(x2, g, w)


def _rope_pair(blk, cosp, sinp):
    return blk * cosp + pltpu.roll(blk, QK_ROPE, axis=1) * sinp


def _mla_proj_kernel(cq_ref, ckv_ref, kr_ref, cos_ref, sin_ref, qg_ref, kvg_ref, wq_ref, wkv_ref,
                     q_ref, kn_ref, krot_ref, v_ref, *, scale):
    cosp = cos_ref[...]
    sinp = sin_ref[...]
    qall = _dot(_rms(cq_ref[...], qg_ref[...]).astype(BF16), wq_ref[...])
    for h in range(MLA_HEADS):
        base = h * Q_HEAD_PAD
        q_ref[:, base:base + QK_NOPE] = (qall[:, base:base + QK_NOPE] * scale).astype(BF16)
        rot = _rope_pair(qall[:, base + QK_NOPE:base + Q_HEAD_PAD], cosp, sinp)
        q_ref[:, base + QK_NOPE:base + Q_HEAD_PAD] = (rot * scale).astype(BF16)
    kvall = _dot(_rms(ckv_ref[...], kvg_ref[...]).astype(BF16), wkv_ref[...])
    kn_ref[...] = kvall[:, :MLA_WIDTH].astype(BF16)
    v_ref[...] = kvall[:, MLA_WIDTH:].astype(BF16)
    krot_ref[...] = _rope_pair(kr_ref[...], cosp, sinp).astype(BF16)


def _mla_proj(u, cosp, sinp, qg, kvg, wq, wkv, seq, tm=256):
    t = u.shape[0]
    nseq = seq // tm
    scale = float((QK_NOPE + QK_ROPE) ** -0.5)
    row = lambda i: (i, 0)
    const = lambda i: (0, 0)
    return pl.pallas_call(
        functools.partial(_mla_proj_kernel, scale=scale),
        grid=(t // tm,),
        in_specs=[
            pl.BlockSpec((tm, Q_LORA), lambda i: (i, U_CQ // Q_LORA)),
            pl.BlockSpec((tm, KV_LORA), lambda i: (i, U_CKV // KV_LORA)),
            pl.BlockSpec((tm, LANE), lambda i: (i, U_KR // LANE)),
            pl.BlockSpec((tm, LANE), lambda i: (i % nseq, 0)),
            pl.BlockSpec((tm, LANE), lambda i: (i % nseq, 0)),
            pl.BlockSpec((1, Q_LORA), const),
            pl.BlockSpec((1, KV_LORA), const),
            pl.BlockSpec((Q_LORA, MLA_HEADS * Q_HEAD_PAD), const),
            pl.BlockSpec((KV_LORA, 2 * MLA_WIDTH), const),
        ],
        out_specs=[
            pl.BlockSpec((tm, MLA_HEADS * Q_HEAD_PAD), row),
            pl.BlockSpec((tm, MLA_WIDTH), row),
            pl.BlockSpec((tm, LANE), row),
            pl.BlockSpec((tm, MLA_WIDTH), row),
        ],
        out_shape=[
            jax.ShapeDtypeStruct((t, MLA_HEADS * Q_HEAD_PAD), BF16),
            jax.ShapeDtypeStruct((t, MLA_WIDTH), BF16),
            jax.ShapeDtypeStruct((t, LANE), BF16),
            jax.ShapeDtypeStruct((t, MLA_WIDTH), BF16),
        ],
        compiler_params=pltpu.CompilerParams(
            dimension_semantics=("parallel",), vmem_limit_bytes=VMEM_LIMIT),
        name="mla_proj",
    )(u, u, u, cosp, sinp, qg, kvg, wq, wkv)


def _attn_kernel(q_ref, kn_ref, kr_ref, v_ref, o_ref, *, tq):
    qi = pl.program_id(1)
    rows = lax.broadcasted_iota(jnp.int32, (tq, tq), 0) // CHUNK
    cols = lax.broadcasted_iota(jnp.int32, (tq, tq), 1) // CHUNK
    diag_mask = cols <= rows

    for h in range(MLA_HEADS):
        q = q_ref[:, h * Q_HEAD_PAD:(h + 1) * Q_HEAD_PAD]

        def step(j, carry, masked, h=h, q=q):
            m, l, acc = carry
            ks = pl.multiple_of(j * tq, tq)
            k = jnp.concatenate(
                [kn_ref[pl.ds(ks, tq), h * QK_NOPE:(h + 1) * QK_NOPE], kr_ref[pl.ds(ks, tq), :]],
                axis=-1)
            s = _dot_nt(q, k)
            if masked:
                s = jnp.where(diag_mask, s, -1e30)
            m_new = jnp.maximum(m, jnp.max(s, axis=-1, keepdims=True))
            alpha = jnp.exp(m - m_new)
            p = jnp.exp(s - m_new)
            l = alpha * l + jnp.sum(p, axis=-1, keepdims=True)
            v = v_ref[pl.ds(ks, tq), h * V_HEAD:(h + 1) * V_HEAD]
            acc = alpha * acc + _dot(p.astype(BF16), v)
            return m_new, l, acc

        init = (jnp.full((tq, 1), -1e30, F32), jnp.zeros((tq, 1), F32), jnp.zeros((tq, V_HEAD), F32))
        carry = lax.fori_loop(0, qi, functools.partial(step, masked=False), init)
        _, l, acc = step(qi, carry, masked=True)
        o_ref[:, h * V_HEAD:(h + 1) * V_HEAD] = (acc / l).astype(BF16)


def _mla_attn(q, kn, kr, v, batch, seq, tq=256):
    t = q.shape[0]
    nq = seq // tq
    return pl.pallas_call(
        functools.partial(_attn_kernel, tq=tq),
        grid=(batch, nq),
        in_specs=[
            pl.BlockSpec((tq, MLA_HEADS * Q_HEAD_PAD), lambda b, i: (b * nq + i, 0)),
            pl.BlockSpec((seq, MLA_WIDTH), lambda b, i: (b, 0)),
            pl.BlockSpec((seq, LANE), lambda b, i: (b, 0)),
            pl.BlockSpec((seq, MLA_WIDTH), lambda b, i: (b, 0)),
        ],
        out_specs=pl.BlockSpec((tq, MLA_WIDTH), lambda b, i: (b * nq + i, 0)),
        out_shape=jax.ShapeDtypeStruct((t, MLA_WIDTH), BF16),
        compiler_params=pltpu.CompilerParams(
            dimension_semantics=("parallel", "arbitrary"), vmem_limit_bytes=VMEM_LIMIT),
        name="mla_attn",
    )(q, kn, kr, v)


def _ssd_kernel(z_ref, xbc_ref, prev_ref, dt_ref, cw_ref, cb_ref, dtb_ref, alog_ref, dexp_ref, ng_ref,
                o_ref, st_ref, xc_ref, dte_ref, acse_ref, *, ts):
    s_idx = pl.program_id(1)
    halo = prev_ref.shape[0]

    @pl.when(s_idx == 0)
    def _():
        st_ref[...] = jnp.zeros_like(st_ref)

    prev = jnp.where(s_idx > 0, prev_ref[...], 0.0)
    xfull = jnp.concatenate([prev, xbc_ref[...]], axis=0)
    conv = cb_ref[...] + cw_ref[SSM_CONV - 1:SSM_CONV, :] * xfull[halo:, :]
    for k in range(SSM_CONV - 1):
        shifted = pltpu.roll(xfull, SSM_CONV - 1 - k, axis=0)[halo:, :]
        conv = conv + cw_ref[k:k + 1, :] * shifted
    xc_ref[...] = conv * _sigmoid(conv)

    raw = dt_ref[...] + dtb_ref[...]
    dt = jnp.maximum(raw, 0.0) + jnp.log1p(jnp.exp(-jnp.abs(raw)))
    a = dt * (-jnp.exp(alog_ref[...]))
    ri = lax.broadcasted_iota(jnp.int32, (ts, ts), 0)
    ci = lax.broadcasted_iota(jnp.int32, (ts, ts), 1)
    tri = jnp.where((ri // CHUNK == ci // CHUNK) & (ci <= ri), 1.0, 0.0).astype(F32)
    acs = _dot_exact(tri, a)
    er = lax.broadcasted_iota(jnp.int32, (LANE, SSM_INNER), 0)
    ec = lax.broadcasted_iota(jnp.int32, (LANE, SSM_INNER), 1) // SSM_HEAD_DIM
    expand = jnp.where(er == ec, 1.0, 0.0).astype(F32)
    dte_ref[...] = _dot_exact(dt, expand)
    acse_ref[...] = _dot_exact(acs, expand)

    li = lax.broadcasted_iota(jnp.int32, (CHUNK, SSM_INNER), 0)
    si = lax.broadcasted_iota(jnp.int32, (CHUNK, SSM_INNER), 1) % SSM_HEAD_DIM
    quad = 4 * SSM_HEAD_DIM
    bd_r = lax.broadcasted_iota(jnp.int32, (quad, quad), 0) // SSM_HEAD_DIM
    bd_c = lax.broadcasted_iota(jnp.int32, (quad, quad), 1) // SSM_HEAD_DIM
    bd_mask = bd_r == bd_c
    dexp = dexp_ref[...]
    ng = ng_ref[...]

    def chunk_body(c, carry):
        r0 = pl.multiple_of(c * CHUNK, CHUNK)
        rows = pl.ds(r0, CHUNK)
        xs = xc_ref[rows, 0:SSM_INNER]
        acx = acse_ref[rows, :]
        last = acx[CHUNK - 1:CHUNK, :]
        xdt = xs * dte_ref[rows, :]
        rv = jnp.sum(jnp.where(li == si, acx, 0.0), axis=0, keepdims=True)
        lmat = jnp.exp(jnp.where(li >= si, acx - rv, -jnp.inf))
        xdec = (xdt * jnp.exp(last - acx)).astype(BF16)
        xdt_b = xdt.astype(BF16)
        eacx = jnp.exp(acx)
        elast = jnp.exp(last)
        ys = []
        for g in range(SSM_GROUPS):
            gl = slice(g * GROUP_WIDTH, (g + 1) * GROUP_WIDTH)
            b_g = xc_ref[rows, SSM_INNER + g * SSM_STATE:SSM_INNER + (g + 1) * SSM_STATE].astype(BF16)
            c0 = SSM_INNER + SSM_GROUPS * SSM_STATE + g * SSM_STATE
            c_g = xc_ref[rows, c0:c0 + SSM_STATE].astype(BF16)
            gt = _dot_nt(c_g, jnp.concatenate([b_g] * 4, axis=0))
            st_g = st_ref[:, gl]
            y_off = _dot(c_g, st_g.astype(BF16)) * eacx[:, gl]
            yd = []
            for qq in range(GROUP_WIDTH // quad):
                sl = slice(g * GROUP_WIDTH + qq * quad, g * GROUP_WIDTH + (qq + 1) * quad)
                m_q = (gt * lmat[:, sl]).astype(BF16)
                x_q = jnp.concatenate([xdt_b[:, sl]] * 4, axis=0)
                x_q = jnp.where(bd_mask, x_q, jnp.zeros_like(x_q))
                yd.append(_dot(m_q, x_q))
            ys.append(jnp.concatenate(yd, axis=-1) + y_off)
            st_ref[:, gl] = st_g * elast[:, gl] + _dot_tn(b_g, xdec[:, gl])
        y = jnp.concatenate(ys, axis=-1) + xs * dexp
        z = z_ref[rows, :]
        y = y * (z * _sigmoid(z))
        outs = []
        for g in range(SSM_GROUPS):
            yg = y[:, g * GROUP_WIDTH:(g + 1) * GROUP_WIDTH]
            outs.append(yg * lax.rsqrt(jnp.mean(yg * yg, axis=-1, keepdims=True) + EPS))
        o_ref[rows, :] = (jnp.concatenate(outs, axis=-1) * ng).astype(BF16)
        return carry

    lax.fori_loop(0, ts // CHUNK, chunk_body, 0)


def _ssd(u, cw, cb, dtb, alog, dexp, ng, batch, seq, ts=256, halo=8):
    t = u.shape[0]
    ns = seq // ts
    const = lambda b, s: (0, 0)
    rowblk = lambda b, s: b * ns + s
    return pl.pallas_call(
        functools.partial(_ssd_kernel, ts=ts),
        grid=(batch, ns),
        in_specs=[
            pl.BlockSpec((ts, SSM_INNER), lambda b, s: (rowblk(b, s), U_Z // SSM_INNER)),
            pl.BlockSpec((ts, SSM_CONV_CH), lambda b, s: (rowblk(b, s), U_XBC // SSM_CONV_CH)),
            pl.BlockSpec((halo, SSM_CONV_CH),
                         lambda b, s: (jnp.maximum(rowblk(b, s) * (ts // halo) - 1, 0), U_XBC // SSM_CONV_CH)),
            pl.BlockSpec((ts, LANE), lambda b, s: (rowblk(b, s), U_DT // LANE)),
            pl.BlockSpec((SSM_CONV, SSM_CONV_CH), const),
            pl.BlockSpec((1, SSM_CONV_CH), const),
            pl.BlockSpec((1, LANE), const),
            pl.BlockSpec((1, LANE), const),
            pl.BlockSpec((1, SSM_INNER), const),
            pl.BlockSpec((1, SSM_INNER), const),
        ],
        out_specs=pl.BlockSpec((ts, SSM_INNER), lambda b, s: (rowblk(b, s), 0)),
        out_shape=jax.ShapeDtypeStruct((t, SSM_INNER), BF16),
        scratch_shapes=[
            pltpu.VMEM((SSM_STATE, SSM_INNER), F32),
            pltpu.VMEM((ts, SSM_CONV_CH), F32),
            pltpu.VMEM((ts, SSM_INNER), F32),
            pltpu.VMEM((ts, SSM_INNER), F32),
        ],
        compiler_params=pltpu.CompilerParams(
            dimension_semantics=("parallel", "arbitrary"), vmem_limit_bytes=VMEM_LIMIT),
        name="ssd",
    )(u, u, u, u, cw, cb, dtb, alog, dexp, ng)


def _out_proj_kernel(a_ref, b_ref, x_ref, wa_ref, wb_ref, g1_ref, g2_ref, x1_ref, h2_ref):
    mix = _dot(a_ref[...], wa_ref[...]) + _dot(b_ref[...], wb_ref[...])
    x1 = x_ref[...] + _rms(mix, g1_ref[...])
    x1_ref[...] = x1
    h2_ref[...] = _rms(x1, g2_ref[...]).astype(BF16)


def _out_proj(a_out, b_out, x2, wa, wb, g1, g2, tm=256):
    t = x2.shape[0]
    row = lambda i: (i, 0)
    const = lambda i: (0, 0)
    return pl.pallas_call(
        _out_proj_kernel,
        grid=(t // tm,),
        in_specs=[
            pl.BlockSpec((tm, MLA_WIDTH), row),
            pl.BlockSpec((tm, SSM_INNER), row),
            pl.BlockSpec((tm, D_MODEL), row),
            pl.BlockSpec((MLA_WIDTH, D_MODEL), const),
            pl.BlockSpec((SSM_INNER, D_MODEL), const),
            pl.BlockSpec((1, D_MODEL), const),
            pl.BlockSpec((1, D_MODEL), const),
        ],
        out_specs=[pl.BlockSpec((tm, D_MODEL), row), pl.BlockSpec((tm, D_MODEL), row)],
        out_shape=[jax.ShapeDtypeStruct((t, D_MODEL), F32), jax.ShapeDtypeStruct((t, D_MODEL), BF16)],
        compiler_params=pltpu.CompilerParams(
            dimension_semantics=("parallel",), vmem_limit_bytes=VMEM_LIMIT),
        name="out_proj",
    )(a_out, b_out, x2, wa, wb, g1, g2)


def _ffn_kernel(h_ref, halo_ref, x1_ref, wg_ref, wu_ref, wd_ref, cw_ref, cb_ref, g_ref, o_ref,
                hcat_ref, acc_ref, *, tm, seq):
    i = pl.program_id(0)
    j = pl.program_id(1)
    halo = halo_ref.shape[0]

    @pl.when(j == 0)
    def _():
        hcat_ref[0:halo, :] = halo_ref[...]
        hcat_ref[halo:, :] = h_ref[...]
        acc_ref[...] = jnp.zeros_like(acc_ref)

    gate = _dot(hcat_ref[...], wg_ref[...])
    up = _dot(hcat_ref[halo:, :], wu_ref[...])
    pos = (i * tm) % seq + lax.broadcasted_iota(jnp.int32, (tm, 1), 0)
    conv = cb_ref[...] + cw_ref[FFN_CONV - 1:FFN_CONV, :] * gate[halo:, :]
    for k in range(FFN_CONV - 1):
        d = FFN_CONV - 1 - k
        shifted = jnp.where(pos >= d, pltpu.roll(gate, d, axis=0)[halo:, :], 0.0)
        conv = conv + cw_ref[k:k + 1, :] * shifted
    c0 = float(np.sqrt(2.0 / np.pi))
    act = 0.5 * conv * (1.0 + jnp.tanh(c0 * (conv + 0.044715 * (conv * conv * conv))))
    acc_ref[...] += _dot((act * up).astype(BF16), wd_ref[...])

    @pl.when(j == pl.num_programs(1) - 1)
    def _():
        o_ref[...] = x1_ref[...] + _rms(acc_ref[...], g_ref[...])


def _ffn(h2, x1, wg, wu, wd, cw, cb, g, seq, tm=512, tf=512, halo=16):
    t = h2.shape[0]
    return pl.pallas_call(
        functools.partial(_ffn_kernel, tm=tm, seq=seq),
        grid=(t // tm, D_FF // tf),
        in_specs=[
            pl.BlockSpec((tm, D_MODEL), lambda i, j: (i, 0)),
            pl.BlockSpec((halo, D_MODEL), lambda i, j: (jnp.maximum(i * (tm // halo) - 1, 0), 0)),
            pl.BlockSpec((tm, D_MODEL), lambda i, j: (i, 0)),
            pl.BlockSpec((D_MODEL, tf), lambda i, j: (0, j)),
            pl.BlockSpec((D_MODEL, tf), lambda i, j: (0, j)),
            pl.BlockSpec((tf, D_MODEL), lambda i, j: (j, 0)),
            pl.BlockSpec((FFN_CONV, tf), lambda i, j: (0, j)),
            pl.BlockSpec((1, tf), lambda i, j: (0, j)),
            pl.BlockSpec((1, D_MODEL), lambda i, j: (0, 0)),
        ],
        out_specs=pl.BlockSpec((tm, D_MODEL), lambda i, j: (i, 0)),
        out_shape=jax.ShapeDtypeStruct((t, D_MODEL), F32),
        scratch_shapes=[pltpu.VMEM((halo + tm, D_MODEL), BF16), pltpu.VMEM((tm, D_MODEL), F32)],
        compiler_params=pltpu.CompilerParams(
            dimension_semantics=("parallel", "arbitrary"), vmem_limit_bytes=VMEM_LIMIT),
        name="ffn",
    )(h2, h2, x1, wg, wu, wd, cw, cb, g)


def _swap_half(w):
    half = w.shape[-1] // 2
    return jnp.concatenate([-w[..., half:], w[..., :half]], axis=-1)


def _rope_tables(seq):
    inv = 1.0 / (ROPE_THETA ** (jnp.arange(0, QK_ROPE, 2, dtype=F32) / QK_ROPE))
    ang = jnp.arange(seq, dtype=F32)[:, None] * inv[None, :]
    zeros = jnp.zeros((seq, LANE - QK_ROPE), F32)
    cosp = jnp.concatenate([jnp.cos(ang), jnp.cos(ang), zeros], axis=-1)
    sinp = jnp.concatenate([jnp.sin(ang), jnp.sin(ang), zeros], axis=-1)
    return cosp, sinp


def _layer(x2, batch, seq, mix_pre_g, w_in, q_norm_g, w_uq, kv_norm_g, w_ukv, ssm_conv_w, ssm_conv_b,
           dt_bias, a_log, d_skip, ssm_norm_g, w_out, mix_post_g, ffn_pre_g, w_gate, w_up,
           ffn_conv_w, ffn_conv_b, w_down, ffn_post_g):
    row = lambda v: v.reshape(1, -1).astype(F32)
    pad_lane = lambda v: jnp.pad(v.astype(F32), (0, LANE - v.shape[0])).reshape(1, LANE)

    cuts = np.cumsum([Q_LORA, KV_LORA, QK_ROPE, SSM_INNER, SSM_CONV_CH]).tolist()
    w_cq, w_ckv, w_kr, w_z, w_xbc, w_dt = jnp.split(w_in, cuts, axis=-1)
    w_in_r = jnp.concatenate(
        [w_cq, w_kr, _swap_half(w_kr), w_dt, jnp.zeros((D_MODEL, LANE - SSM_HEADS), w_in.dtype),
         w_ckv, w_xbc, w_z], axis=-1).astype(BF16)

    wq3 = w_uq.reshape(Q_LORA, MLA_HEADS, QK_NOPE + QK_ROPE)
    wq_rope = wq3[..., QK_NOPE:]
    wq_r = jnp.concatenate([wq3[..., :QK_NOPE], wq_rope, _swap_half(wq_rope)], axis=-1)
    wq_r = wq_r.reshape(Q_LORA, MLA_HEADS * Q_HEAD_PAD).astype(BF16)
    wkv3 = w_ukv.reshape(KV_LORA, MLA_HEADS, QK_NOPE + V_HEAD)
    wkv_r = jnp.concatenate([wkv3[..., :QK_NOPE].reshape(KV_LORA, MLA_WIDTH),
                             wkv3[..., QK_NOPE:].reshape(KV_LORA, MLA_WIDTH)], axis=-1).astype(BF16)

    u = _in_proj(x2, row(mix_pre_g), w_in_r)
    cosp, sinp = _rope_tables(seq)
    q, kn, kr, v = _mla_proj(u, cosp, sinp, row(q_norm_g), row(kv_norm_g), wq_r, wkv_r, seq)
    a_out = _mla_attn(q, kn, kr, v, batch, seq)
    b_out = _ssd(u, ssm_conv_w.astype(F32), row(ssm_conv_b), pad_lane(dt_bias), pad_lane(a_log),
                 row(jnp.repeat(d_skip, SSM_HEAD_DIM)), row(ssm_norm_g), batch, seq)
    w_out_b = w_out.astype(BF16)
    x1, h2 = _out_proj(a_out, b_out, x2, w_out_b[:MLA_WIDTH], w_out_b[MLA_WIDTH:],
                       row(mix_post_g), row(ffn_pre_g))
    return _ffn(h2, x1, w_gate.astype(BF16), w_up.astype(BF16), w_down.astype(BF16),
                ffn_conv_w.astype(F32), row(ffn_conv_b), row(ffn_post_g), seq)


def kernel(x, mix_pre_g, w_in, q_norm_g, w_uq, kv_norm_g, w_ukv, ssm_conv_w, ssm_conv_b, dt_bias, a_log,
           d_skip, ssm_norm_g, w_out, mix_post_g, ffn_pre_g, w_gate, w_up, ffn_conv_w, ffn_conv_b,
           w_down, ffn_post_g):
    batch, seq, _ = x.shape
    x2 = x.reshape(batch * seq, D_MODEL)
    for l in range(mix_pre_g.shape[0]):
        x2 = _layer(x2, batch, seq, mix_pre_g[l], w_in[l], q_norm_g[l], w_uq[l], kv_norm_g[l], w_ukv[l],
                    ssm_conv_w[l], ssm_conv_b[l], dt_bias[l], a_log[l], d_skip[l], ssm_norm_g[l],
                    w_out[l], mix_post_g[l], ffn_pre_g[l], w_gate[l], w_up[l], ffn_conv_w[l],
                    ffn_conv_b[l], w_down[l], ffn_post_g[l])
    return x2.reshape(batch, seq, D_MODEL)
```

```python
import functools

import jax
import jax.numpy as jnp
import numpy as np
from jax import lax
from jax.experimental import pallas as pl
from jax.experimental.pallas import tpu as pltpu

F32 = jnp.float32
BF16 = jnp.bfloat16

D_MODEL = 2048
CHUNK = 64
EPS = 1e-6

MLA_HEADS = 8
Q_LORA = 768
KV_LORA = 512
QK_NOPE = 128
QK_ROPE = 64
V_HEAD = 128
ROPE_THETA = 10000.0
MLA_WIDTH = MLA_HEADS * V_HEAD
Q_HEAD_PAD = 256

SSM_HEADS = 16
SSM_HEAD_DIM = 64
SSM_INNER = SSM_HEADS * SSM_HEAD_DIM
SSM_GROUPS = 2
SSM_STATE = 128
SSM_CONV = 4
SSM_CONV_CH = SSM_INNER + 2 * SSM_GROUPS * SSM_STATE
GROUP_WIDTH = SSM_INNER // SSM_GROUPS

D_FF = 5632
FFN_CONV = 3

LANE = 128
U_COLS = 4096
U_CQ = 0
U_KR = Q_LORA
U_DT = U_KR + LANE
U_CKV = U_DT + LANE
U_XBC = U_CKV + KV_LORA
U_Z = U_XBC + SSM_CONV_CH

VMEM_LIMIT = 56 * 1024 * 1024


def _rms(x, g):
    return x * lax.rsqrt(jnp.mean(x * x, axis=-1, keepdims=True) + EPS) * g


def _sigmoid(x):
    return 1.0 / (1.0 + jnp.exp(-x))


def _dot(a, b):
    return jnp.dot(a, b, preferred_element_type=F32)


def _dot_nt(a, b):
    return lax.dot_general(a, b, (((1,), (1,)), ((), ())), preferred_element_type=F32)


def _dot_tn(a, b):
    return lax.dot_general(a, b, (((0,), (0,)), ((), ())), preferred_element_type=F32)


def _split3(x, lane):
    xm = jnp.where(lane < SSM_HEADS, x, 0.0)
    hi = xm.astype(BF16).astype(F32)
    r1 = xm - hi
    mid = r1.astype(BF16).astype(F32)
    lo = (r1 - mid).astype(BF16).astype(F32)
    packed = hi + pltpu.roll(mid, SSM_HEADS, axis=1) + pltpu.roll(lo, 2 * SSM_HEADS, axis=1)
    return packed.astype(BF16)


def _in_proj_kernel(x_ref, g_ref, w_ref, o_ref, h_ref):
    @pl.when(pl.program_id(1) == 0)
    def _():
        h_ref[...] = _rms(x_ref[...], g_ref[...]).astype(BF16)

    o_ref[...] = _dot(h_ref[...], w_ref[...])


def _in_proj(x2, g, w, tm=512, tn=1024):
    t = x2.shape[0]
    return pl.pallas_call(
        _in_proj_kernel,
        grid=(t // tm, U_COLS // tn),
        in_specs=[
            pl.BlockSpec((tm, D_MODEL), lambda i, j: (i, 0)),
            pl.BlockSpec((1, D_MODEL), lambda i, j: (0, 0)),
            pl.BlockSpec((D_MODEL, tn), lambda i, j: (0, j)),
        ],
        out_specs=pl.BlockSpec((tm, tn), lambda i, j: (i, j)),
        out_shape=jax.ShapeDtypeStruct((t, U_COLS), F32),
        scratch_shapes=[pltpu.VMEM((tm, D_MODEL), BF16)],
        compiler_params=pltpu.CompilerParams(
            dimension_semantics=("parallel", "arbitrary"), vmem_limit_bytes=VMEM_LIMIT),
        name="in_proj",
    )(x2, g, w)


def _rope_pair(blk, cosp, sinp):
    return blk * cosp + pltpu.roll(blk, QK_ROPE, axis=1) * sinp


def _mla_proj_kernel(cq_ref, ckv_ref, kr_ref, cos_ref, sin_ref, qg_ref, kvg_ref, wq_ref, wk_ref, wvt_ref,
                     q_ref, kn_ref, krot_ref, vt_ref, *, scale):
    cosp = cos_ref[...]
    sinp = sin_ref[...]
    qall = _dot(_rms(cq_ref[...], qg_ref[...]).astype(BF16), wq_ref[...])
    for h in range(MLA_HEADS):
        base = h * Q_HEAD_PAD
        q_ref[:, base:base + QK_NOPE] = (qall[:, base:base + QK_NOPE] * scale).astype(BF16)
        rot = _rope_pair(qall[:, base + QK_NOPE:base + Q_HEAD_PAD], cosp, sinp)
        q_ref[:, base + QK_NOPE:base + Q_HEAD_PAD] = (rot * scale).astype(BF16)
    ckv = _rms(ckv_ref[...], kvg_ref[...]).astype(BF16)
    kn_ref[...] = _dot(ckv, wk_ref[...]).astype(BF16)
    vt_ref[...] = _dot_nt(wvt_ref[...], ckv).astype(BF16)
    krot_ref[...] = _rope_pair(kr_ref[...], cosp, sinp).astype(BF16)


def _mla_proj(u, cosp, sinp, qg, kvg, wq, wk, wvt, seq, tm=256):
    t = u.shape[0]
    nseq = seq // tm
    scale = float((QK_NOPE + QK_ROPE) ** -0.5 * np.log2(np.e))
    row = lambda i: (i, 0)
    const = lambda i: (0, 0)
    return pl.pallas_call(
        functools.partial(_mla_proj_kernel, scale=scale),
        grid=(t // tm,),
        in_specs=[
            pl.BlockSpec((tm, Q_LORA), lambda i: (i, U_CQ // Q_LORA)),
            pl.BlockSpec((tm, KV_LORA), lambda i: (i, U_CKV // KV_LORA)),
            pl.BlockSpec((tm, LANE), lambda i: (i, U_KR // LANE)),
            pl.BlockSpec((tm, LANE), lambda i: (i % nseq, 0)),
            pl.BlockSpec((tm, LANE), lambda i: (i % nseq, 0)),
            pl.BlockSpec((1, Q_LORA), const),
            pl.BlockSpec((1, KV_LORA), const),
            pl.BlockSpec((Q_LORA, MLA_HEADS * Q_HEAD_PAD), const),
            pl.BlockSpec((KV_LORA, MLA_WIDTH), const),
            pl.BlockSpec((MLA_WIDTH, KV_LORA), const),
        ],
        out_specs=[
            pl.BlockSpec((tm, MLA_HEADS * Q_HEAD_PAD), row),
            pl.BlockSpec((tm, MLA_WIDTH), row),
            pl.BlockSpec((tm, LANE), row),
            pl.BlockSpec((MLA_WIDTH, tm), lambda i: (0, i)),
        ],
        out_shape=[
            jax.ShapeDtypeStruct((t, MLA_HEADS * Q_HEAD_PAD), BF16),
            jax.ShapeDtypeStruct((t, MLA_WIDTH), BF16),
            jax.ShapeDtypeStruct((t, LANE), BF16),
            jax.ShapeDtypeStruct((MLA_WIDTH, t), BF16),
        ],
        compiler_params=pltpu.CompilerParams(
            dimension_semantics=("parallel",), vmem_limit_bytes=VMEM_LIMIT),
        name="mla_proj",
    )(u, u, u, cosp, sinp, qg, kvg, wq, wk, wvt)


ONES_ROWS = 16


def _attn_kernel(q_ref, kn_ref, kr_ref, vt_ref, o_ref, m_ref, acc_ref, *, tq):
    qi = pl.program_id(1)
    krow = lax.broadcasted_iota(jnp.int32, (tq, tq), 0) // CHUNK
    qcol = lax.broadcasted_iota(jnp.int32, (tq, tq), 1) // CHUNK
    diag_mask = krow <= qcol
    ones = jnp.ones((ONES_ROWS, tq), BF16)

    m_ref[...] = jnp.full(m_ref.shape, -1e30, F32)
    acc_ref[...] = jnp.zeros_like(acc_ref)

    def block(j, masked):
        ks = pl.multiple_of(j * tq, tq)
        kr = kr_ref[pl.ds(ks, tq), :]
        for h in range(MLA_HEADS):
            q = q_ref[:, h * Q_HEAD_PAD:(h + 1) * Q_HEAD_PAD]
            k = jnp.concatenate([kn_ref[pl.ds(ks, tq), h * QK_NOPE:(h + 1) * QK_NOPE], kr], axis=-1)
            s = _dot_nt(k, q)
            if masked:
                s = jnp.where(diag_mask, s, -1e30)
            m = m_ref[h]
            m_new = jnp.maximum(m, jnp.max(s, axis=0, keepdims=True))
            alpha = jnp.exp2(m - m_new)
            p = jnp.exp2(s - m_new).astype(BF16)
            vt = jnp.concatenate([vt_ref[h * V_HEAD:(h + 1) * V_HEAD, pl.ds(ks, tq)], ones], axis=0)
            acc_ref[h] = alpha * acc_ref[h] + _dot(vt, p)
            m_ref[h] = m_new

    def loop_body(j, carry):
        block(j, masked=False)
        return carry

    lax.fori_loop(0, qi, loop_body, 0)
    block(qi, masked=True)
    for h in range(MLA_HEADS):
        acc = acc_ref[h]
        out_t = acc[:V_HEAD, :] / acc[V_HEAD:V_HEAD + 1, :]
        o_ref[:, h * V_HEAD:(h + 1) * V_HEAD] = out_t.T.astype(BF16)


def _mla_attn(q, kn, kr, vt, batch, seq, tq=256):
    t = q.shape[0]
    nq = seq // tq
    return pl.pallas_call(
        functools.partial(_attn_kernel, tq=tq),
        grid=(batch, nq),
        in_specs=[
            pl.BlockSpec((tq, MLA_HEADS * Q_HEAD_PAD), lambda b, i: (b * nq + i, 0)),
            pl.BlockSpec((seq, MLA_WIDTH), lambda b, i: (b, 0)),
            pl.BlockSpec((seq, LANE), lambda b, i: (b, 0)),
            pl.BlockSpec((MLA_WIDTH, seq), lambda b, i: (0, b)),
        ],
        out_specs=pl.BlockSpec((tq, MLA_WIDTH), lambda b, i: (b * nq + i, 0)),
        out_shape=jax.ShapeDtypeStruct((t, MLA_WIDTH), BF16),
        scratch_shapes=[
            pltpu.VMEM((MLA_HEADS, 1, tq), F32),
            pltpu.VMEM((MLA_HEADS, V_HEAD + ONES_ROWS, tq), F32),
        ],
        compiler_params=pltpu.CompilerParams(
            dimension_semantics=("parallel", "arbitrary"), vmem_limit_bytes=VMEM_LIMIT),
        name="mla_attn",
    )(q, kn, kr, vt)


def _ssd_kernel(z_ref, xbc_ref, prev_ref, dt_ref, cw_ref, cb_ref, dtb_ref, alog_ref, dexp_ref, ng_ref,
                o_ref, st_ref, xc_ref, dte_ref, acse_ref, *, ts):
    s_idx = pl.program_id(1)
    halo = prev_ref.shape[0]

    @pl.when(s_idx == 0)
    def _():
        st_ref[...] = jnp.zeros_like(st_ref)

    sub = lax.broadcasted_iota(jnp.int32, (halo, SSM_CONV_CH), 0)

    def conv_tile(i, prev):
        r = pl.multiple_of(i * halo, halo)
        cur = xbc_ref[pl.ds(r, halo), :]
        conv = cb_ref[...] + cw_ref[SSM_CONV - 1:SSM_CONV, :] * cur
        for d in range(1, SSM_CONV):
            shifted = jnp.where(sub < d, pltpu.roll(prev, d, axis=0), pltpu.roll(cur, d, axis=0))
            conv = conv + cw_ref[SSM_CONV - 1 - d:SSM_CONV - d, :] * shifted
        xc_ref[pl.ds(r, halo), :] = conv * _sigmoid(conv)
        return cur

    lax.fori_loop(0, ts // halo, conv_tile, jnp.where(s_idx > 0, prev_ref[...], 0.0), unroll=4)

    raw = dt_ref[...] + dtb_ref[...]
    dt = jnp.maximum(raw, 0.0) + jnp.log1p(jnp.exp(-jnp.abs(raw)))
    a = dt * (-jnp.exp(alog_ref[...]))
    lane = lax.broadcasted_iota(jnp.int32, (ts, LANE), 1)
    ri = lax.broadcasted_iota(jnp.int32, (ts, ts), 0)
    ci = lax.broadcasted_iota(jnp.int32, (ts, ts), 1)
    tri = jnp.where((ri // CHUNK == ci // CHUNK) & (ci <= ri), 1.0, 0.0).astype(BF16)
    c3 = _dot(tri, _split3(a, lane))
    acs = c3 + pltpu.roll(c3, LANE - SSM_HEADS, axis=1) + pltpu.roll(c3, LANE - 2 * SSM_HEADS, axis=1)
    er = lax.broadcasted_iota(jnp.int32, (LANE, SSM_INNER), 0)
    ec = lax.broadcasted_iota(jnp.int32, (LANE, SSM_INNER), 1) // SSM_HEAD_DIM
    expand = jnp.where((er % SSM_HEADS == ec) & (er < 3 * SSM_HEADS), 1.0, 0.0).astype(BF16)
    both = _dot(jnp.concatenate([_split3(dt, lane), _split3(acs, lane)], axis=0), expand)
    dte_ref[...] = both[:ts]
    acse_ref[...] = both[ts:]

    li = lax.broadcasted_iota(jnp.int32, (CHUNK, SSM_INNER), 0)
    si = lax.broadcasted_iota(jnp.int32, (CHUNK, SSM_INNER), 1) % SSM_HEAD_DIM
    quad = 4 * SSM_HEAD_DIM
    bd_r = lax.broadcasted_iota(jnp.int32, (quad, quad), 0) // SSM_HEAD_DIM
    bd_c = lax.broadcasted_iota(jnp.int32, (quad, quad), 1) // SSM_HEAD_DIM
    bd_mask = bd_r == bd_c
    dexp = dexp_ref[...]
    ng = ng_ref[...]

    def chunk_body(c, carry):
        r0 = pl.multiple_of(c * CHUNK, CHUNK)
        rows = pl.ds(r0, CHUNK)
        xs = xc_ref[rows, 0:SSM_INNER]
        acx = acse_ref[rows, :]
        last = acx[CHUNK - 1:CHUNK, :]
        xdt = xs * dte_ref[rows, :]
        rv = jnp.sum(jnp.where(li == si, acx, 0.0), axis=0, keepdims=True)
        lmat = jnp.exp(jnp.where(li >= si, acx - rv, -jnp.inf))
        xdec = (xdt * jnp.exp(last - acx)).astype(BF16)
        xdt_b = xdt.astype(BF16)
        eacx = jnp.exp(acx)
        elast = jnp.exp(last)
        ys = []
        for g in range(SSM_GROUPS):
            gl = slice(g * GROUP_WIDTH, (g + 1) * GROUP_WIDTH)
            b_g = xc_ref[rows, SSM_INNER + g * SSM_STATE:SSM_INNER + (g + 1) * SSM_STATE].astype(BF16)
            c0 = SSM_INNER + SSM_GROUPS * SSM_STATE + g * SSM_STATE
            c_g = xc_ref[rows, c0:c0 + SSM_STATE].astype(BF16)
            gt = _dot_nt(c_g, jnp.concatenate([b_g] * 4, axis=0))
            st_g = st_ref[:, gl]
            y_off = _dot(c_g, st_g.astype(BF16)) * eacx[:, gl]
            yd = []
            for qq in range(GROUP_WIDTH // quad):
                sl = slice(g * GROUP_WIDTH + qq * quad, g * GROUP_WIDTH + (qq + 1) * quad)
                m_q = (gt * lmat[:, sl]).astype(BF16)
                x_q = jnp.concatenate([xdt_b[:, sl]] * 4, axis=0)
                x_q = jnp.where(bd_mask, x_q, jnp.zeros_like(x_q))
                yd.append(_dot(m_q, x_q))
            ys.append(jnp.concatenate(yd, axis=-1) + y_off)
            st_ref[:, gl] = st_g * elast[:, gl] + _dot_tn(b_g, xdec[:, gl])
        y = jnp.concatenate(ys, axis=-1) + xs * dexp
        z = z_ref[rows, :]
        y = y * (z * _sigmoid(z))
        outs = []
        for g in range(SSM_GROUPS):
            yg = y[:, g * GROUP_WIDTH:(g + 1) * GROUP_WIDTH]
            outs.append(yg * lax.rsqrt(jnp.mean(yg * yg, axis=-1, keepdims=True) + EPS))
        o_ref[rows, :] = (jnp.concatenate(outs, axis=-1) * ng).astype(BF16)
        return carry

    lax.fori_loop(0, ts // CHUNK, chunk_body, 0)


def _ssd(u, cw, cb, dtb, alog, dexp, ng, batch, seq, ts=256, halo=8):
    t = u.shape[0]
    ns = seq // ts
    const = lambda b, s: (0, 0)
    rowblk = lambda b, s: b * ns + s
    return pl.pallas_call(
        functools.partial(_ssd_kernel, ts=ts),
        grid=(batch, ns),
        in_specs=[
            pl.BlockSpec((ts, SSM_INNER), lambda b, s: (rowblk(b, s), U_Z // SSM_INNER)),
            pl.BlockSpec((ts, SSM_CONV_CH), lambda b, s: (rowblk(b, s), U_XBC // SSM_CONV_CH)),
            pl.BlockSpec((halo, SSM_CONV_CH),
                         lambda b, s: (jnp.maximum(rowblk(b, s) * (ts // halo) - 1, 0), U_XBC // SSM_CONV_CH)),
            pl.BlockSpec((ts, LANE), lambda b, s: (rowblk(b, s), U_DT // LANE)),
            pl.BlockSpec((SSM_CONV, SSM_CONV_CH), const),
            pl.BlockSpec((1, SSM_CONV_CH), const),
            pl.BlockSpec((1, LANE), const),
            pl.BlockSpec((1, LANE), const),
            pl.BlockSpec((1, SSM_INNER), const),
            pl.BlockSpec((1, SSM_INNER), const),
        ],
        out_specs=pl.BlockSpec((ts, SSM_INNER), lambda b, s: (rowblk(b, s), 0)),
        out_shape=jax.ShapeDtypeStruct((t, SSM_INNER), BF16),
        scratch_shapes=[
            pltpu.VMEM((SSM_STATE, SSM_INNER), F32),
            pltpu.VMEM((ts, SSM_CONV_CH), F32),
            pltpu.VMEM((ts, SSM_INNER), F32),
            pltpu.VMEM((ts, SSM_INNER), F32),
        ],
        compiler_params=pltpu.CompilerParams(
            dimension_semantics=("parallel", "arbitrary"), vmem_limit_bytes=VMEM_LIMIT),
        name="ssd",
    )(u, u, u, u, cw, cb, dtb, alog, dexp, ng)


def _out_proj_kernel(a_ref, b_ref, x_ref, wa_ref, wb_ref, g1_ref, g2_ref, x1_ref, h2_ref):
    mix = _dot(a_ref[...], wa_ref[...]) + _dot(b_ref[...], wb_ref[...])
    x1 = x_ref[...] + _rms(mix, g1_ref[...])
    x1_ref[...] = x1
    h2_ref[...] = _rms(x1, g2_ref[...]).astype(BF16)


def _out_proj(a_out, b_out, x2, wa, wb, g1, g2, tm=256):
    t = x2.shape[0]
    row = lambda i: (i, 0)
    const = lambda i: (0, 0)
    return pl.pallas_call(
        _out_proj_kernel,
        grid=(t // tm,),
        in_specs=[
            pl.BlockSpec((tm, MLA_WIDTH), row),
            pl.BlockSpec((tm, SSM_INNER), row),
            pl.BlockSpec((tm, D_MODEL), row),
            pl.BlockSpec((MLA_WIDTH, D_MODEL), const),
            pl.BlockSpec((SSM_INNER, D_MODEL), const),
            pl.BlockSpec((1, D_MODEL), const),
            pl.BlockSpec((1, D_MODEL), const),
        ],
        out_specs=[pl.BlockSpec((tm, D_MODEL), row), pl.BlockSpec((tm, D_MODEL), row)],
        out_shape=[jax.ShapeDtypeStruct((t, D_MODEL), F32), jax.ShapeDtypeStruct((t, D_MODEL), BF16)],
        compiler_params=pltpu.CompilerParams(
            dimension_semantics=("parallel",), vmem_limit_bytes=VMEM_LIMIT),
        name="out_proj",
    )(a_out, b_out, x2, wa, wb, g1, g2)


def _ffn_kernel(h_ref, halo_ref, x1_ref, wg_ref, wu_ref, wd_ref, cw_ref, cb_ref, g_ref, o_ref,
                hcat_ref, acc_ref, *, tm, seq):
    i = pl.program_id(0)
    j = pl.program_id(1)
    halo = halo_ref.shape[0]

    @pl.when(j == 0)
    def _():
        hcat_ref[0:halo, :] = halo_ref[...]
        hcat_ref[halo:, :] = h_ref[...]
        acc_ref[...] = jnp.zeros_like(acc_ref)

    gate = _dot(hcat_ref[...], wg_ref[...])
    up = _dot(hcat_ref[halo:, :], wu_ref[...])
    pos = (i * tm) % seq + lax.broadcasted_iota(jnp.int32, (tm, 1), 0)
    conv = cb_ref[...] + cw_ref[FFN_CONV - 1:FFN_CONV, :] * gate[halo:, :]
    for k in range(FFN_CONV - 1):
        d = FFN_CONV - 1 - k
        shifted = jnp.where(pos >= d, pltpu.roll(gate, d, axis=0)[halo:, :], 0.0)
        conv = conv + cw_ref[k:k + 1, :] * shifted
    c0 = float(np.sqrt(2.0 / np.pi))
    act = 0.5 * conv * (1.0 + jnp.tanh(c0 * (conv + 0.044715 * (conv * conv * conv))))
    acc_ref[...] += _dot((act * up).astype(BF16), wd_ref[...])

    @pl.when(j == pl.num_programs(1) - 1)
    def _():
        o_ref[...] = x1_ref[...] + _rms(acc_ref[...], g_ref[...])


def _ffn(h2, x1, wg, wu, wd, cw, cb, g, seq, tm=512, tf=512, halo=16):
    t = h2.shape[0]
    return pl.pallas_call(
        functools.partial(_ffn_kernel, tm=tm, seq=seq),
        grid=(t // tm, D_FF // tf),
        in_specs=[
            pl.BlockSpec((tm, D_MODEL), lambda i, j: (i, 0)),
            pl.BlockSpec((halo, D_MODEL), lambda i, j: (jnp.maximum(i * (tm // halo) - 1, 0), 0)),
            pl.BlockSpec((tm, D_MODEL), lambda i, j: (i, 0)),
            pl.BlockSpec((D_MODEL, tf), lambda i, j: (0, j)),
            pl.BlockSpec((D_MODEL, tf), lambda i, j: (0, j)),
            pl.BlockSpec((tf, D_MODEL), lambda i, j: (j, 0)),
            pl.BlockSpec((FFN_CONV, tf), lambda i, j: (0, j)),
            pl.BlockSpec((1, tf), lambda i, j: (0, j)),
            pl.BlockSpec((1, D_MODEL), lambda i, j: (0, 0)),
        ],
        out_specs=pl.BlockSpec((tm, D_MODEL), lambda i, j: (i, 0)),
        out_shape=jax.ShapeDtypeStruct((t, D_MODEL), F32),
        scratch_shapes=[pltpu.VMEM((halo + tm, D_MODEL), BF16), pltpu.VMEM((tm, D_MODEL), F32)],
        compiler_params=pltpu.CompilerParams(
            dimension_semantics=("parallel", "arbitrary"), vmem_limit_bytes=VMEM_LIMIT),
        name="ffn",
    )(h2, h2, x1, wg, wu, wd, cw, cb, g)


def _swap_half(w):
    half = w.shape[-1] // 2
    return jnp.concatenate([-w[..., half:], w[..., :half]], axis=-1)


def _rope_tables(seq):
    inv = 1.0 / (ROPE_THETA ** (jnp.arange(0, QK_ROPE, 2, dtype=F32) / QK_ROPE))
    ang = jnp.arange(seq, dtype=F32)[:, None] * inv[None, :]
    zeros = jnp.zeros((seq, LANE - QK_ROPE), F32)
    cosp = jnp.concatenate([jnp.cos(ang), jnp.cos(ang), zeros], axis=-1)
    sinp = jnp.concatenate([jnp.sin(ang), jnp.sin(ang), zeros], axis=-1)
    return cosp, sinp


def _layer(x2, batch, seq, mix_pre_g, w_in, q_norm_g, w_uq, kv_norm_g, w_ukv, ssm_conv_w, ssm_conv_b,
           dt_bias, a_log, d_skip, ssm_norm_g, w_out, mix_post_g, ffn_pre_g, w_gate, w_up,
           ffn_conv_w, ffn_conv_b, w_down, ffn_post_g):
    row = lambda v: v.reshape(1, -1).astype(F32)
    pad_lane = lambda v: jnp.pad(v.astype(F32), (0, LANE - v.shape[0])).reshape(1, LANE)

    cuts = np.cumsum([Q_LORA, KV_LORA, QK_ROPE, SSM_INNER, SSM_CONV_CH]).tolist()
    w_cq, w_ckv, w_kr, w_z, w_xbc, w_dt = jnp.split(w_in, cuts, axis=-1)
    w_in_r = jnp.concatenate(
        [w_cq, w_kr, _swap_half(w_kr), w_dt, jnp.zeros((D_MODEL, LANE - SSM_HEADS), w_in.dtype),
         w_ckv, w_xbc, w_z], axis=-1).astype(BF16)

    wq3 = w_uq.reshape(Q_LORA, MLA_HEADS, QK_NOPE + QK_ROPE)
    wq_rope = wq3[..., QK_NOPE:]
    wq_r = jnp.concatenate([wq3[..., :QK_NOPE], wq_rope, _swap_half(wq_rope)], axis=-1)
    wq_r = wq_r.reshape(Q_LORA, MLA_HEADS * Q_HEAD_PAD).astype(BF16)
    wkv3 = w_ukv.reshape(KV_LORA, MLA_HEADS, QK_NOPE + V_HEAD)
    wk_r = wkv3[..., :QK_NOPE].reshape(KV_LORA, MLA_WIDTH).astype(BF16)
    wvt_r = wkv3[..., QK_NOPE:].reshape(KV_LORA, MLA_WIDTH).T.astype(BF16)

    u = _in_proj(x2, row(mix_pre_g), w_in_r)
    cosp, sinp = _rope_tables(seq)
    q, kn, kr, vt = _mla_proj(u, cosp, sinp, row(q_norm_g), row(kv_norm_g), wq_r, wk_r, wvt_r, seq)
    a_out = _mla_attn(q, kn, kr, vt, batch, seq)
    b_out = _ssd(u, ssm_conv_w.astype(F32), row(ssm_conv_b), pad_lane(dt_bias), pad_lane(a_log),
                 row(jnp.repeat(d_skip, SSM_HEAD_DIM)), row(ssm_norm_g), batch, seq)
    w_out_b = w_out.astype(BF16)
    x1, h2 = _out_proj(a_out, b_out, x2, w_out_b[:MLA_WIDTH], w_out_b[MLA_WIDTH:],
                       row(mix_post_g), row(ffn_pre_g))
    return _ffn(h2, x1, w_gate.astype(BF16), w_up.astype(BF16), w_down.astype(BF16),
                ffn_conv_w.astype(F32), row(ffn_conv_b), row(ffn_post_g), seq)


def kernel(x, mix_pre_g, w_in, q_norm_g, w_uq, kv_norm_g, w_ukv, ssm_conv_w, ssm_conv_b, dt_bias, a_log,
           d_skip, ssm_norm_g, w_out, mix_post_g, ffn_pre_g, w_gate, w_up, ffn_conv_w, ffn_conv_b,
           w_down, ffn_post_g):
    batch, seq, _ = x.shape
    x2 = x.reshape(batch * seq, D_MODEL)
    for l in range(mix_pre_g.shape[0]):
        x2 = _layer(x2, batch, seq, mix_pre_g[l], w_in[l], q_norm_g[l], w_uq[l], kv_norm_g[l], w_ukv[l],
                    ssm_conv_w[l], ssm_conv_b[l], dt_bias[l], a_log[l], d_skip[l], ssm_norm_g[l],
                    w_out[l], mix_post_g[l], ffn_pre_g[l], w_gate[l], w_up[l], ffn_conv_w[l],
                    ffn_conv_b[l], w_down[l], ffn_post_g[l])
    return x2.reshape(batch, seq, D_MODEL)
```

```python
import functools

import jax
import jax.numpy as jnp
import numpy as np
from jax import lax
from jax.experimental import pallas as pl
from jax.experimental.pallas import tpu as pltpu

F32 = jnp.float32
BF16 = jnp.bfloat16

D_MODEL = 2048
CHUNK = 64
EPS = 1e-6

MLA_HEADS = 8
Q_LORA = 768
KV_LORA = 512
QK_NOPE = 128
QK_ROPE = 64
V_HEAD = 128
ROPE_THETA = 10000.0
MLA_WIDTH = MLA_HEADS * V_HEAD
Q_HEAD_PAD = 256

SSM_HEADS = 16
SSM_HEAD_DIM = 64
SSM_INNER = SSM_HEADS * SSM_HEAD_DIM
SSM_GROUPS = 2
SSM_STATE = 128
SSM_CONV = 4
SSM_CONV_CH = SSM_INNER + 2 * SSM_GROUPS * SSM_STATE
GROUP_WIDTH = SSM_INNER // SSM_GROUPS

D_FF = 5632
FFN_CONV = 3

LANE = 128
U_COLS = 4096
U_CQ = 0
U_KR = Q_LORA
U_DT = U_KR + LANE
U_CKV = U_DT + LANE
U_XBC = U_CKV + KV_LORA
U_Z = U_XBC + SSM_CONV_CH

VMEM_LIMIT = 56 * 1024 * 1024


def _rms(x, g):
    return x * lax.rsqrt(jnp.mean(x * x, axis=-1, keepdims=True) + EPS) * g


def _sigmoid(x):
    return 1.0 / (1.0 + jnp.exp(-x))


def _dot(a, b):
    return jnp.dot(a, b, preferred_element_type=F32)


def _dot_nt(a, b):
    return lax.dot_general(a, b, (((1,), (1,)), ((), ())), preferred_element_type=F32)


def _dot_tn(a, b):
    return lax.dot_general(a, b, (((0,), (0,)), ((), ())), preferred_element_type=F32)


def _split3(x, lane):
    xm = jnp.where(lane < SSM_HEADS, x, 0.0)
    hi = xm.astype(BF16).astype(F32)
    r1 = xm - hi
    mid = r1.astype(BF16).astype(F32)
    lo = (r1 - mid).astype(BF16).astype(F32)
    packed = hi + pltpu.roll(mid, SSM_HEADS, axis=1) + pltpu.roll(lo, 2 * SSM_HEADS, axis=1)
    return packed.astype(BF16)


def _in_proj_kernel(x_ref, g_ref, w_ref, o_ref):
    x = x_ref[...]
    r = lax.rsqrt(jnp.mean(x * x, axis=-1, keepdims=True) + EPS)
    o_ref[...] = _dot((x * g_ref[...]).astype(BF16), w_ref[...]) * r


def _in_proj(x2, g, w, tm=512):
    t = x2.shape[0]
    return pl.pallas_call(
        _in_proj_kernel,
        grid=(t // tm,),
        in_specs=[
            pl.BlockSpec((tm, D_MODEL), lambda i: (i, 0)),
            pl.BlockSpec((1, D_MODEL), lambda i: (0, 0)),
            pl.BlockSpec((D_MODEL, U_COLS), lambda i: (0, 0), pipeline_mode=pl.Buffered(1)),
        ],
        out_specs=pl.BlockSpec((tm, U_COLS), lambda i: (i, 0)),
        out_shape=jax.ShapeDtypeStruct((t, U_COLS), F32),
        compiler_params=pltpu.CompilerParams(
            dimension_semantics=("parallel",), vmem_limit_bytes=VMEM_LIMIT),
        name="in_proj",
    )(x2, g, w)


def _rope_pair(blk, cosp, sinp):
    return blk * cosp + pltpu.roll(blk, QK_ROPE, axis=1) * sinp


def _mla_proj_kernel(cq_ref, ckv_ref, kr_ref, cos_ref, sin_ref, qg_ref, kvg_ref, wq_ref, wk_ref, wvt_ref,
                     q_ref, kn_ref, krot_ref, vt_ref, *, scale):
    cosp = cos_ref[...]
    sinp = sin_ref[...]
    qall = _dot(_rms(cq_ref[...], qg_ref[...]).astype(BF16), wq_ref[...])
    for h in range(MLA_HEADS):
        base = h * Q_HEAD_PAD
        q_ref[:, base:base + QK_NOPE] = (qall[:, base:base + QK_NOPE] * scale).astype(BF16)
        rot = _rope_pair(qall[:, base + QK_NOPE:base + Q_HEAD_PAD], cosp, sinp)
        q_ref[:, base + QK_NOPE:base + Q_HEAD_PAD] = (rot * scale).astype(BF16)
    ckv = _rms(ckv_ref[...], kvg_ref[...]).astype(BF16)
    kn_ref[...] = _dot(ckv, wk_ref[...]).astype(BF16)
    vt_ref[...] = _dot_nt(wvt_ref[...], ckv).astype(BF16)
    krot_ref[...] = _rope_pair(kr_ref[...], cosp, sinp).astype(BF16)


def _mla_proj(u, cosp, sinp, qg, kvg, wq, wk, wvt, seq, tm=256):
    t = u.shape[0]
    nseq = seq // tm
    scale = float((QK_NOPE + QK_ROPE) ** -0.5 * np.log2(np.e))
    row = lambda i: (i, 0)
    const = lambda i: (0, 0)
    return pl.pallas_call(
        functools.partial(_mla_proj_kernel, scale=scale),
        grid=(t // tm,),
        in_specs=[
            pl.BlockSpec((tm, Q_LORA), lambda i: (i, U_CQ // Q_LORA)),
            pl.BlockSpec((tm, KV_LORA), lambda i: (i, U_CKV // KV_LORA)),
            pl.BlockSpec((tm, LANE), lambda i: (i, U_KR // LANE)),
            pl.BlockSpec((tm, LANE), lambda i: (i % nseq, 0)),
            pl.BlockSpec((tm, LANE), lambda i: (i % nseq, 0)),
            pl.BlockSpec((1, Q_LORA), const),
            pl.BlockSpec((1, KV_LORA), const),
            pl.BlockSpec((Q_LORA, MLA_HEADS * Q_HEAD_PAD), const),
            pl.BlockSpec((KV_LORA, MLA_WIDTH), const),
            pl.BlockSpec((MLA_WIDTH, KV_LORA), const),
        ],
        out_specs=[
            pl.BlockSpec((tm, MLA_HEADS * Q_HEAD_PAD), row),
            pl.BlockSpec((tm, MLA_WIDTH), row),
            pl.BlockSpec((tm, LANE), row),
            pl.BlockSpec((MLA_WIDTH, tm), lambda i: (0, i)),
        ],
        out_shape=[
            jax.ShapeDtypeStruct((t, MLA_HEADS * Q_HEAD_PAD), BF16),
            jax.ShapeDtypeStruct((t, MLA_WIDTH), BF16),
            jax.ShapeDtypeStruct((t, LANE), BF16),
            jax.ShapeDtypeStruct((MLA_WIDTH, t), BF16),
        ],
        compiler_params=pltpu.CompilerParams(
            dimension_semantics=("parallel",), vmem_limit_bytes=VMEM_LIMIT),
        name="mla_proj",
    )(u, u, u, cosp, sinp, qg, kvg, wq, wk, wvt)


ONES_ROWS = 16


def _attn_kernel(q_ref, kn_ref, kr_ref, vt_ref, o_ref, s_ref, mrun_ref, acc_ref, *, tq):
    qi = pl.program_id(1)
    sublanes = mrun_ref.shape[1]
    krow = lax.broadcasted_iota(jnp.int32, (tq, tq), 0) // CHUNK
    qcol = lax.broadcasted_iota(jnp.int32, (tq, tq), 1) // CHUNK
    diag_mask = krow <= qcol
    ones = jnp.ones((ONES_ROWS, tq), BF16)

    mrun_ref[...] = jnp.full(mrun_ref.shape, -1e30, F32)
    acc_ref[...] = jnp.zeros_like(acc_ref)

    def scores(j, masked):
        ks = pl.multiple_of(j * tq, tq)
        kr = kr_ref[pl.ds(ks, tq), :]
        for h in range(MLA_HEADS):
            q = q_ref[:, h * Q_HEAD_PAD:(h + 1) * Q_HEAD_PAD]
            k = jnp.concatenate([kn_ref[pl.ds(ks, tq), h * QK_NOPE:(h + 1) * QK_NOPE], kr], axis=-1)
            s = _dot_nt(k, q)
            if masked:
                s = jnp.where(diag_mask, s, -1e30)
            s_ref[h, j] = s
            tile_max = jnp.max(s.reshape(tq // sublanes, sublanes, tq), axis=0)
            mrun_ref[h] = jnp.maximum(mrun_ref[h], tile_max)

    def scores_body(j, carry):
        scores(j, masked=False)
        return carry

    lax.fori_loop(0, qi, scores_body, 0)
    scores(qi, masked=True)

    m = [jnp.max(mrun_ref[h], axis=0, keepdims=True) for h in range(MLA_HEADS)]

    def accumulate(j, carry):
        ks = pl.multiple_of(j * tq, tq)
        for h in range(MLA_HEADS):
            p = jnp.exp2(s_ref[h, j] - m[h]).astype(BF16)
            vt = jnp.concatenate([vt_ref[h * V_HEAD:(h + 1) * V_HEAD, pl.ds(ks, tq)], ones], axis=0)
            acc_ref[h] += _dot(vt, p)
        return carry

    lax.fori_loop(0, qi + 1, accumulate, 0)
    for h in range(MLA_HEADS):
        acc = acc_ref[h]
        out_t = acc[:V_HEAD, :] / acc[V_HEAD:V_HEAD + 1, :]
        o_ref[:, h * V_HEAD:(h + 1) * V_HEAD] = out_t.T.astype(BF16)


def _mla_attn(q, kn, kr, vt, batch, seq, tq=256):
    t = q.shape[0]
    nq = seq // tq
    return pl.pallas_call(
        functools.partial(_attn_kernel, tq=tq),
        grid=(batch, nq),
        in_specs=[
            pl.BlockSpec((tq, MLA_HEADS * Q_HEAD_PAD), lambda b, i: (b * nq + i, 0)),
            pl.BlockSpec((seq, MLA_WIDTH), lambda b, i: (b, 0)),
            pl.BlockSpec((seq, LANE), lambda b, i: (b, 0)),
            pl.BlockSpec((MLA_WIDTH, seq), lambda b, i: (0, b)),
        ],
        out_specs=pl.BlockSpec((tq, MLA_WIDTH), lambda b, i: (b * nq + i, 0)),
        out_shape=jax.ShapeDtypeStruct((t, MLA_WIDTH), BF16),
        scratch_shapes=[
            pltpu.VMEM((MLA_HEADS, nq, tq, tq), F32),
            pltpu.VMEM((MLA_HEADS, 8, tq), F32),
            pltpu.VMEM((MLA_HEADS, V_HEAD + ONES_ROWS, tq), F32),
        ],
        compiler_params=pltpu.CompilerParams(
            dimension_semantics=("parallel", "arbitrary"), vmem_limit_bytes=VMEM_LIMIT),
        name="mla_attn",
    )(q, kn, kr, vt)


def _ssd_kernel(z_ref, xbc_ref, prev_ref, dt_ref, cw_ref, cb_ref, dtb_ref, alog_ref, dexp_ref, ng_ref,
                o_ref, st_ref, xc_ref, dte_ref, acse_ref, wb_ref, *, ts):
    s_idx = pl.program_id(1)
    halo = prev_ref.shape[0]

    @pl.when(s_idx == 0)
    def _():
        st_ref[...] = jnp.zeros_like(st_ref)

    sub = lax.broadcasted_iota(jnp.int32, (halo, SSM_CONV_CH), 0)
    for k in range(SSM_CONV):
        wb_ref[k] = jnp.broadcast_to(cw_ref[k:k + 1, :], (halo, SSM_CONV_CH))
    wb_ref[SSM_CONV] = jnp.broadcast_to(cb_ref[...], (halo, SSM_CONV_CH))

    def conv_tile(i, prev):
        r = pl.multiple_of(i * halo, halo)
        cur = xbc_ref[pl.ds(r, halo), :]
        conv = wb_ref[SSM_CONV] + wb_ref[SSM_CONV - 1] * cur
        for d in range(1, SSM_CONV):
            shifted = jnp.where(sub < d, pltpu.roll(prev, d, axis=0), pltpu.roll(cur, d, axis=0))
            conv = conv + wb_ref[SSM_CONV - 1 - d] * shifted
        xc_ref[pl.ds(r, halo), :] = conv * _sigmoid(conv)
        return cur

    lax.fori_loop(0, ts // halo, conv_tile, jnp.where(s_idx > 0, prev_ref[...], 0.0), unroll=4)

    raw = dt_ref[...] + dtb_ref[...]
    dt = jnp.maximum(raw, 0.0) + jnp.log1p(jnp.exp(-jnp.abs(raw)))
    a = dt * (-jnp.exp(alog_ref[...]))
    lane = lax.broadcasted_iota(jnp.int32, (ts, LANE), 1)
    ri = lax.broadcasted_iota(jnp.int32, (ts, ts), 0)
    ci = lax.broadcasted_iota(jnp.int32, (ts, ts), 1)
    tri = jnp.where((ri // CHUNK == ci // CHUNK) & (ci <= ri), 1.0, 0.0).astype(BF16)
    c3 = _dot(tri, _split3(a, lane))
    acs = c3 + pltpu.roll(c3, LANE - SSM_HEADS, axis=1) + pltpu.roll(c3, LANE - 2 * SSM_HEADS, axis=1)
    er = lax.broadcasted_iota(jnp.int32, (LANE, SSM_INNER), 0)
    ec = lax.broadcasted_iota(jnp.int32, (LANE, SSM_INNER), 1) // SSM_HEAD_DIM
    expand = jnp.where((er % SSM_HEADS == ec) & (er < 3 * SSM_HEADS), 1.0, 0.0).astype(BF16)
    log2e = float(np.log2(np.e))
    both = _dot(jnp.concatenate([_split3(dt, lane), _split3(acs * log2e, lane)], axis=0), expand)
    dte_ref[...] = both[:ts]
    acse_ref[...] = both[ts:]

    li = lax.broadcasted_iota(jnp.int32, (CHUNK, SSM_INNER), 0)
    si = lax.broadcasted_iota(jnp.int32, (CHUNK, SSM_INNER), 1) % SSM_HEAD_DIM
    quad = 4 * SSM_HEAD_DIM
    bd_r = lax.broadcasted_iota(jnp.int32, (quad, quad), 0) // SSM_HEAD_DIM
    bd_c = lax.broadcasted_iota(jnp.int32, (quad, quad), 1) // SSM_HEAD_DIM
    bd_mask = bd_r == bd_c
    dexp = dexp_ref[...]
    ng = ng_ref[...]

    def chunk_body(c, carry):
        r0 = pl.multiple_of(c * CHUNK, CHUNK)
        rows = pl.ds(r0, CHUNK)
        xs = xc_ref[rows, 0:SSM_INNER]
        acx = acse_ref[rows, :]
        last = acx[CHUNK - 1:CHUNK, :]
        xdt = xs * dte_ref[rows, :]
        rv = jnp.sum(jnp.where(li == si, acx, 0.0), axis=0, keepdims=True)
        lmat = jnp.exp2(jnp.where(li >= si, acx - rv, -jnp.inf))
        xdec = (xdt * jnp.exp2(last - acx)).astype(BF16)
        xdt_b = xdt.astype(BF16)
        eacx = jnp.exp2(acx)
        elast = jnp.exp2(last)
        ys = []
        for g in range(SSM_GROUPS):
            gl = slice(g * GROUP_WIDTH, (g + 1) * GROUP_WIDTH)
            b_g = xc_ref[rows, SSM_INNER + g * SSM_STATE:SSM_INNER + (g + 1) * SSM_STATE].astype(BF16)
            c0 = SSM_INNER + SSM_GROUPS * SSM_STATE + g * SSM_STATE
            c_g = xc_ref[rows, c0:c0 + SSM_STATE].astype(BF16)
            gt = _dot_nt(c_g, jnp.concatenate([b_g] * 4, axis=0))
            st_g = st_ref[:, gl]
            y_off = _dot(c_g, st_g.astype(BF16)) * eacx[:, gl]
            yd = []
            for qq in range(GROUP_WIDTH // quad):
                sl = slice(g * GROUP_WIDTH + qq * quad, g * GROUP_WIDTH + (qq + 1) * quad)
                m_q = (gt * lmat[:, sl]).astype(BF16)
                x_q = jnp.concatenate([xdt_b[:, sl]] * 4, axis=0)
                x_q = jnp.where(bd_mask, x_q, jnp.zeros_like(x_q))
                yd.append(_dot(m_q, x_q))
            ys.append(jnp.concatenate(yd, axis=-1) + y_off)
            st_ref[:, gl] = st_g * elast[:, gl] + _dot_tn(b_g, xdec[:, gl])
        y = jnp.concatenate(ys, axis=-1) + xs * dexp
        z = z_ref[rows, :]
        y = y * (z * _sigmoid(z))
        outs = []
        for g in range(SSM_GROUPS):
            yg = y[:, g * GROUP_WIDTH:(g + 1) * GROUP_WIDTH]
            outs.append(yg * lax.rsqrt(jnp.mean(yg * yg, axis=-1, keepdims=True) + EPS))
        o_ref[rows, :] = (jnp.concatenate(outs, axis=-1) * ng).astype(BF16)
        return carry

    lax.fori_loop(0, ts // CHUNK, chunk_body, 0, unroll=True)


def _ssd(u, cw, cb, dtb, alog, dexp, ng, batch, seq, ts=256, halo=8):
    t = u.shape[0]
    ns = seq // ts
    const = lambda b, s: (0, 0)
    rowblk = lambda b, s: b * ns + s
    return pl.pallas_call(
        functools.partial(_ssd_kernel, ts=ts),
        grid=(batch, ns),
        in_specs=[
            pl.BlockSpec((ts, SSM_INNER), lambda b, s: (rowblk(b, s), U_Z // SSM_INNER)),
            pl.BlockSpec((ts, SSM_CONV_CH), lambda b, s: (rowblk(b, s), U_XBC // SSM_CONV_CH)),
            pl.BlockSpec((halo, SSM_CONV_CH),
                         lambda b, s: (jnp.maximum(rowblk(b, s) * (ts // halo) - 1, 0), U_XBC // SSM_CONV_CH)),
            pl.BlockSpec((ts, LANE), lambda b, s: (rowblk(b, s), U_DT // LANE)),
            pl.BlockSpec((SSM_CONV, SSM_CONV_CH), const),
            pl.BlockSpec((1, SSM_CONV_CH), const),
            pl.BlockSpec((1, LANE), const),
            pl.BlockSpec((1, LANE), const),
            pl.BlockSpec((1, SSM_INNER), const),
            pl.BlockSpec((1, SSM_INNER), const),
        ],
        out_specs=pl.BlockSpec((ts, SSM_INNER), lambda b, s: (rowblk(b, s), 0)),
        out_shape=jax.ShapeDtypeStruct((t, SSM_INNER), BF16),
        scratch_shapes=[
            pltpu.VMEM((SSM_STATE, SSM_INNER), F32),
            pltpu.VMEM((ts, SSM_CONV_CH), F32),
            pltpu.VMEM((ts, SSM_INNER), F32),
            pltpu.VMEM((ts, SSM_INNER), F32),
            pltpu.VMEM((SSM_CONV + 1, halo, SSM_CONV_CH), F32),
        ],
        compiler_params=pltpu.CompilerParams(
            dimension_semantics=("parallel", "arbitrary"), vmem_limit_bytes=VMEM_LIMIT),
        name="ssd",
    )(u, u, u, u, cw, cb, dtb, alog, dexp, ng)


def _out_proj_kernel(a_ref, b_ref, x_ref, wa_ref, wb_ref, g1_ref, g2_ref, x1_ref, h2_ref, *, piece):
    for r in range(0, a_ref.shape[0], piece):
        rows = slice(r, r + piece)
        mix = _dot(a_ref[rows, :], wa_ref[...]) + _dot(b_ref[rows, :], wb_ref[...])
        x1 = x_ref[rows, :] + _rms(mix, g1_ref[...])
        x1_ref[rows, :] = x1
        h2_ref[rows, :] = _rms(x1, g2_ref[...]).astype(BF16)


def _out_proj(a_out, b_out, x2, wa, wb, g1, g2, tm=512, piece=256):
    t = x2.shape[0]
    row = lambda i: (i, 0)
    const = lambda i: (0, 0)
    return pl.pallas_call(
        functools.partial(_out_proj_kernel, piece=piece),
        grid=(t // tm,),
        in_specs=[
            pl.BlockSpec((tm, MLA_WIDTH), row),
            pl.BlockSpec((tm, SSM_INNER), row),
            pl.BlockSpec((tm, D_MODEL), row),
            pl.BlockSpec((MLA_WIDTH, D_MODEL), const),
            pl.BlockSpec((SSM_INNER, D_MODEL), const),
            pl.BlockSpec((1, D_MODEL), const),
            pl.BlockSpec((1, D_MODEL), const),
        ],
        out_specs=[pl.BlockSpec((tm, D_MODEL), row), pl.BlockSpec((tm, D_MODEL), row)],
        out_shape=[jax.ShapeDtypeStruct((t, D_MODEL), F32), jax.ShapeDtypeStruct((t, D_MODEL), BF16)],
        compiler_params=pltpu.CompilerParams(
            dimension_semantics=("parallel",), vmem_limit_bytes=VMEM_LIMIT),
        name="out_proj",
    )(a_out, b_out, x2, wa, wb, g1, g2)


def _ffn_kernel(h_ref, x1_ref, wg_ref, wu_ref, wd_ref, cw_ref, cb_ref, g_ref, o_ref,
                acc_ref, act_ref, tail_ref, *, tm, seq):
    i = pl.program_id(0)
    j = pl.program_id(1)
    nf = tail_ref.shape[0]
    last = nf
    sub = tail_ref.shape[1]

    def produce(slot):
        gate = _dot(h_ref[...], wg_ref[...])
        up = _dot(h_ref[...], wu_ref[...])
        prev = tail_ref[j]
        tail_ref[j] = gate[tm - sub:, :]
        pos = (i * tm) % seq + lax.broadcasted_iota(jnp.int32, (tm, 1), 0)
        row = lax.broadcasted_iota(jnp.int32, prev.shape, 0)
        conv = cb_ref[...] + cw_ref[FFN_CONV - 1:FFN_CONV, :] * gate
        for k in range(FFN_CONV - 1):
            d = FFN_CONV - 1 - k
            rolled = pltpu.roll(gate, d, axis=0)
            head = jnp.where(row < d, pltpu.roll(prev, d, axis=0), rolled[:sub, :])
            shifted = jnp.concatenate([head, rolled[sub:, :]], axis=0)
            conv = conv + cw_ref[k:k + 1, :] * jnp.where(pos >= d, shifted, 0.0)
        c0 = float(np.sqrt(2.0 / np.pi))
        act = 0.5 * conv * (1.0 + jnp.tanh(c0 * (conv + 0.044715 * (conv * conv * conv))))
        act_ref[slot] = (act * up).astype(BF16)

    def consume(slot):
        acc_ref[...] += _dot(act_ref[slot], wd_ref[...])

    @pl.when((i == 0) & (j == 0))
    def _():
        tail_ref[...] = jnp.zeros_like(tail_ref)

    @pl.when(j == 0)
    def _():
        acc_ref[...] = jnp.zeros_like(acc_ref)
        produce(0)

    for parity in range(2):
        @pl.when((j > 0) & (j < last) & (j % 2 == parity))
        def _(parity=parity):
            produce(parity)
            consume(1 - parity)

    @pl.when(j == last)
    def _():
        consume((nf - 1) % 2)
        o_ref[...] = x1_ref[...] + _rms(acc_ref[...], g_ref[...])


def _ffn(h2, x1, wg, wu, wd, cw, cb, g, seq, tm=512, tf=512):
    t = h2.shape[0]
    nf = D_FF // tf
    cur = lambda i, j: (0, jnp.minimum(j, nf - 1))
    return pl.pallas_call(
        functools.partial(_ffn_kernel, tm=tm, seq=seq),
        grid=(t // tm, nf + 1),
        in_specs=[
            pl.BlockSpec((tm, D_MODEL), lambda i, j: (i, 0)),
            pl.BlockSpec((tm, D_MODEL), lambda i, j: (i, 0)),
            pl.BlockSpec((D_MODEL, tf), cur),
            pl.BlockSpec((D_MODEL, tf), cur),
            pl.BlockSpec((tf, D_MODEL), lambda i, j: (jnp.maximum(j - 1, 0), 0)),
            pl.BlockSpec((FFN_CONV, tf), cur),
            pl.BlockSpec((1, tf), cur),
            pl.BlockSpec((1, D_MODEL), lambda i, j: (0, 0)),
        ],
        out_specs=pl.BlockSpec((tm, D_MODEL), lambda i, j: (i, 0)),
        out_shape=jax.ShapeDtypeStruct((t, D_MODEL), F32),
        scratch_shapes=[
            pltpu.VMEM((tm, D_MODEL), F32),
            pltpu.VMEM((2, tm, tf), BF16),
            pltpu.VMEM((nf, 8, tf), F32),
        ],
        compiler_params=pltpu.CompilerParams(
            dimension_semantics=("arbitrary", "arbitrary"), vmem_limit_bytes=VMEM_LIMIT),
        name="ffn",
    )(h2, x1, wg, wu, wd, cw, cb, g)


def _swap_half(w):
    half = w.shape[-1] // 2
    return jnp.concatenate([-w[..., half:], w[..., :half]], axis=-1)


def _rope_tables(seq):
    inv = 1.0 / (ROPE_THETA ** (jnp.arange(0, QK_ROPE, 2, dtype=F32) / QK_ROPE))
    ang = jnp.arange(seq, dtype=F32)[:, None] * inv[None, :]
    zeros = jnp.zeros((seq, LANE - QK_ROPE), F32)
    cosp = jnp.concatenate([jnp.cos(ang), jnp.cos(ang), zeros], axis=-1)
    sinp = jnp.concatenate([jnp.sin(ang), jnp.sin(ang), zeros], axis=-1)
    return cosp, sinp


def _layer(x2, batch, seq, mix_pre_g, w_in, q_norm_g, w_uq, kv_norm_g, w_ukv, ssm_conv_w, ssm_conv_b,
           dt_bias, a_log, d_skip, ssm_norm_g, w_out, mix_post_g, ffn_pre_g, w_gate, w_up,
           ffn_conv_w, ffn_conv_b, w_down, ffn_post_g):
    row = lambda v: v.reshape(1, -1).astype(F32)
    pad_lane = lambda v: jnp.pad(v.astype(F32), (0, LANE - v.shape[0])).reshape(1, LANE)

    cuts = np.cumsum([Q_LORA, KV_LORA, QK_ROPE, SSM_INNER, SSM_CONV_CH]).tolist()
    w_cq, w_ckv, w_kr, w_z, w_xbc, w_dt = jnp.split(w_in.astype(BF16), cuts, axis=-1)
    w_in_r = jnp.concatenate(
        [w_cq, w_kr, _swap_half(w_kr), w_dt, jnp.zeros((D_MODEL, LANE - SSM_HEADS), BF16),
         w_ckv, w_xbc, w_z], axis=-1)

    wq3 = w_uq.reshape(Q_LORA, MLA_HEADS, QK_NOPE + QK_ROPE)
    wq_rope = wq3[..., QK_NOPE:]
    wq_r = jnp.concatenate([wq3[..., :QK_NOPE], wq_rope, _swap_half(wq_rope)], axis=-1)
    wq_r = wq_r.reshape(Q_LORA, MLA_HEADS * Q_HEAD_PAD).astype(BF16)
    wkv3 = w_ukv.reshape(KV_LORA, MLA_HEADS, QK_NOPE + V_HEAD)
    wk_r = wkv3[..., :QK_NOPE].reshape(KV_LORA, MLA_WIDTH).astype(BF16)
    wvt_r = wkv3[..., QK_NOPE:].reshape(KV_LORA, MLA_WIDTH).T.astype(BF16)

    u = _in_proj(x2, row(mix_pre_g), w_in_r)
    cosp, sinp = _rope_tables(seq)
    q, kn, kr, vt = _mla_proj(u, cosp, sinp, row(q_norm_g), row(kv_norm_g), wq_r, wk_r, wvt_r, seq)
    a_out = _mla_attn(q, kn, kr, vt, batch, seq)
    b_out = _ssd(u, ssm_conv_w.astype(F32), row(ssm_conv_b), pad_lane(dt_bias), pad_lane(a_log),
                 row(jnp.repeat(d_skip, SSM_HEAD_DIM)), row(ssm_norm_g), batch, seq)
    w_out_b = w_out.astype(BF16)
    x1, h2 = _out_proj(a_out, b_out, x2, w_out_b[:MLA_WIDTH], w_out_b[MLA_WIDTH:],
                       row(mix_post_g), row(ffn_pre_g))
    return _ffn(h2, x1, w_gate.astype(BF16), w_up.astype(BF16), w_down.astype(BF16),
                ffn_conv_w.astype(F32), row(ffn_conv_b), row(ffn_post_g), seq)


def kernel(x, mix_pre_g, w_in, q_norm_g, w_uq, kv_norm_g, w_ukv, ssm_conv_w, ssm_conv_b, dt_bias, a_log,
           d_skip, ssm_norm_g, w_out, mix_post_g, ffn_pre_g, w_gate, w_up, ffn_conv_w, ffn_conv_b,
           w_down, ffn_post_g):
    batch, seq, _ = x.shape
    x2 = x.reshape(batch * seq, D_MODEL)
    for l in range(mix_pre_g.shape[0]):
        x2 = _layer(x2, batch, seq, mix_pre_g[l], w_in[l], q_norm_g[l], w_uq[l], kv_norm_g[l], w_ukv[l],
                    ssm_conv_w[l], ssm_conv_b[l], dt_bias[l], a_log[l], d_skip[l], ssm_norm_g[l],
                    w_out[l], mix_post_g[l], ffn_pre_g[l], w_gate[l], w_up[l], ffn_conv_w[l],
                    ffn_conv_b[l], w_down[l], ffn_post_g[l])
    return x2.reshape(batch, seq, D_MODEL)
```

```python
import functools

import jax
import jax.numpy as jnp
import numpy as np
from jax import lax
from jax.experimental import pallas as pl
from jax.experimental.pallas import tpu as pltpu

F32 = jnp.float32
BF16 = jnp.bfloat16

D_MODEL = 2048
CHUNK = 64
EPS = 1e-6

MLA_HEADS = 8
Q_LORA = 768
KV_LORA = 512
QK_NOPE = 128
QK_ROPE = 64
V_HEAD = 128
ROPE_THETA = 10000.0
MLA_WIDTH = MLA_HEADS * V_HEAD
Q_HEAD_PAD = 256

SSM_HEADS = 16
SSM_HEAD_DIM = 64
SSM_INNER = SSM_HEADS * SSM_HEAD_DIM
SSM_GROUPS = 2
SSM_STATE = 128
SSM_CONV = 4
SSM_CONV_CH = SSM_INNER + 2 * SSM_GROUPS * SSM_STATE
GROUP_WIDTH = SSM_INNER // SSM_GROUPS

D_FF = 5632
FFN_CONV = 3
FFN_ROWS = 256

LANE = 128
U_COLS = 4096
U_CQ = 0
U_KR = Q_LORA
U_DT = U_KR + LANE
U_CKV = U_DT + LANE
U_XBC = U_CKV + KV_LORA
U_Z = U_XBC + SSM_CONV_CH

VMEM_LIMIT = 56 * 1024 * 1024


def _rms(x, g):
    return x * lax.rsqrt(jnp.mean(x * x, axis=-1, keepdims=True) + EPS) * g


def _sigmoid(x):
    return 1.0 / (1.0 + jnp.exp(-x))


def _dot(a, b):
    return jnp.dot(a, b, preferred_element_type=F32)


def _dot_nt(a, b):
    return lax.dot_general(a, b, (((1,), (1,)), ((), ())), preferred_element_type=F32)


def _dot_tn(a, b):
    return lax.dot_general(a, b, (((0,), (0,)), ((), ())), preferred_element_type=F32)


def _split3(x, lane):
    xm = jnp.where(lane < SSM_HEADS, x, 0.0)
    hi = xm.astype(BF16).astype(F32)
    r1 = xm - hi
    mid = r1.astype(BF16).astype(F32)
    lo = (r1 - mid).astype(BF16).astype(F32)
    packed = hi + pltpu.roll(mid, SSM_HEADS, axis=1) + pltpu.roll(lo, 2 * SSM_HEADS, axis=1)
    return packed.astype(BF16)


def _in_proj_kernel(x_ref, g_ref, w_ref, o_ref):
    x = x_ref[...]
    r = lax.rsqrt(jnp.mean(x * x, axis=-1, keepdims=True) + EPS)
    o_ref[...] = _dot((x * g_ref[...]).astype(BF16), w_ref[...]) * r


def _in_proj(x2, g, w, tm=512):
    t = x2.shape[0]
    return pl.pallas_call(
        _in_proj_kernel,
        grid=(t // tm,),
        in_specs=[
            pl.BlockSpec((tm, D_MODEL), lambda i: (i, 0)),
            pl.BlockSpec((1, D_MODEL), lambda i: (0, 0)),
            pl.BlockSpec((D_MODEL, U_COLS), lambda i: (0, 0), pipeline_mode=pl.Buffered(1)),
        ],
        out_specs=pl.BlockSpec((tm, U_COLS), lambda i: (i, 0)),
        out_shape=jax.ShapeDtypeStruct((t, U_COLS), F32),
        compiler_params=pltpu.CompilerParams(
            dimension_semantics=("parallel",), vmem_limit_bytes=VMEM_LIMIT),
        name="in_proj",
    )(x2, g, w)


def _rope_pair(blk, cosp, sinp):
    return blk * cosp + pltpu.roll(blk, QK_ROPE, axis=1) * sinp


def _mla_proj_kernel(cq_ref, ckv_ref, kr_ref, cos_ref, sin_ref, qg_ref, kvg_ref, wq_ref, wk_ref, wvt_ref,
                     q_ref, kn_ref, krot_ref, vt_ref, *, scale):
    cosp = cos_ref[...]
    sinp = sin_ref[...]
    qall = _dot(_rms(cq_ref[...], qg_ref[...]).astype(BF16), wq_ref[...])
    for h in range(MLA_HEADS):
        base = h * Q_HEAD_PAD
        q_ref[:, base:base + QK_NOPE] = (qall[:, base:base + QK_NOPE] * scale).astype(BF16)
        rot = _rope_pair(qall[:, base + QK_NOPE:base + Q_HEAD_PAD], cosp, sinp)
        q_ref[:, base + QK_NOPE:base + Q_HEAD_PAD] = (rot * scale).astype(BF16)
    ckv = _rms(ckv_ref[...], kvg_ref[...]).astype(BF16)
    kn_ref[...] = _dot(ckv, wk_ref[...]).astype(BF16)
    vt_ref[...] = _dot_nt(wvt_ref[...], ckv).astype(BF16)
    krot_ref[...] = _rope_pair(kr_ref[...], cosp, sinp).astype(BF16)


def _mla_proj(u, cosp, sinp, qg, kvg, wq, wk, wvt, seq, tm=256):
    t = u.shape[0]
    nseq = seq // tm
    scale = float((QK_NOPE + QK_ROPE) ** -0.5 * np.log2(np.e))
    row = lambda i: (i, 0)
    const = lambda i: (0, 0)
    return pl.pallas_call(
        functools.partial(_mla_proj_kernel, scale=scale),
        grid=(t // tm,),
        in_specs=[
            pl.BlockSpec((tm, Q_LORA), lambda i: (i, U_CQ // Q_LORA)),
            pl.BlockSpec((tm, KV_LORA), lambda i: (i, U_CKV // KV_LORA)),
            pl.BlockSpec((tm, LANE), lambda i: (i, U_KR // LANE)),
            pl.BlockSpec((tm, LANE), lambda i: (i % nseq, 0)),
            pl.BlockSpec((tm, LANE), lambda i: (i % nseq, 0)),
            pl.BlockSpec((1, Q_LORA), const),
            pl.BlockSpec((1, KV_LORA), const),
            pl.BlockSpec((Q_LORA, MLA_HEADS * Q_HEAD_PAD), const),
            pl.BlockSpec((KV_LORA, MLA_WIDTH), const),
            pl.BlockSpec((MLA_WIDTH, KV_LORA), const),
        ],
        out_specs=[
            pl.BlockSpec((tm, MLA_HEADS * Q_HEAD_PAD), row),
            pl.BlockSpec((tm, MLA_WIDTH), row),
            pl.BlockSpec((tm, LANE), row),
            pl.BlockSpec((MLA_WIDTH, tm), lambda i: (0, i)),
        ],
        out_shape=[
            jax.ShapeDtypeStruct((t, MLA_HEADS * Q_HEAD_PAD), BF16),
            jax.ShapeDtypeStruct((t, MLA_WIDTH), BF16),
            jax.ShapeDtypeStruct((t, LANE), BF16),
            jax.ShapeDtypeStruct((MLA_WIDTH, t), BF16),
        ],
        compiler_params=pltpu.CompilerParams(
            dimension_semantics=("parallel",), vmem_limit_bytes=VMEM_LIMIT),
        name="mla_proj",
    )(u, u, u, cosp, sinp, qg, kvg, wq, wk, wvt)


ONES_ROWS = 16


def _attn_kernel(q_ref, kn_ref, kr_ref, vt_ref, o_ref, s_ref, mrun_ref, acc_ref, *, tq):
    qi = pl.program_id(1)
    sublanes = mrun_ref.shape[1]
    krow = lax.broadcasted_iota(jnp.int32, (tq, tq), 0) // CHUNK
    qcol = lax.broadcasted_iota(jnp.int32, (tq, tq), 1) // CHUNK
    diag_mask = krow <= qcol
    ones = jnp.ones((ONES_ROWS, tq), BF16)

    mrun_ref[...] = jnp.full(mrun_ref.shape, -1e30, F32)
    acc_ref[...] = jnp.zeros_like(acc_ref)

    def scores(j, masked):
        ks = pl.multiple_of(j * tq, tq)
        kr = kr_ref[pl.ds(ks, tq), :]
        for h in range(MLA_HEADS):
            q = q_ref[:, h * Q_HEAD_PAD:(h + 1) * Q_HEAD_PAD]
            k = jnp.concatenate([kn_ref[pl.ds(ks, tq), h * QK_NOPE:(h + 1) * QK_NOPE], kr], axis=-1)
            s = _dot_nt(k, q)
            if masked:
                s = jnp.where(diag_mask, s, -1e30)
            s_ref[h, j] = s
            tile_max = jnp.max(s.reshape(tq // sublanes, sublanes, tq), axis=0)
            mrun_ref[h] = jnp.maximum(mrun_ref[h], tile_max)

    def scores_body(j, carry):
        scores(j, masked=False)
        return carry

    lax.fori_loop(0, qi, scores_body, 0)
    scores(qi, masked=True)

    m = [jnp.max(mrun_ref[h], axis=0, keepdims=True) for h in range(MLA_HEADS)]

    def accumulate(j, carry):
        ks = pl.multiple_of(j * tq, tq)
        for h in range(MLA_HEADS):
            p = jnp.exp2(s_ref[h, j] - m[h]).astype(BF16)
            vt = jnp.concatenate([vt_ref[h * V_HEAD:(h + 1) * V_HEAD, pl.ds(ks, tq)], ones], axis=0)
            acc_ref[h] += _dot(vt, p)
        return carry

    lax.fori_loop(0, qi + 1, accumulate, 0)
    for h in range(MLA_HEADS):
        acc = acc_ref[h]
        out_t = acc[:V_HEAD, :] / acc[V_HEAD:V_HEAD + 1, :]
        o_ref[:, h * V_HEAD:(h + 1) * V_HEAD] = out_t.T.astype(BF16)


def _mla_attn(q, kn, kr, vt, batch, seq, tq=256):
    t = q.shape[0]
    nq = seq // tq
    return pl.pallas_call(
        functools.partial(_attn_kernel, tq=tq),
        grid=(batch, nq),
        in_specs=[
            pl.BlockSpec((tq, MLA_HEADS * Q_HEAD_PAD), lambda b, i: (b * nq + i, 0)),
            pl.BlockSpec((seq, MLA_WIDTH), lambda b, i: (b, 0)),
            pl.BlockSpec((seq, LANE), lambda b, i: (b, 0)),
            pl.BlockSpec((MLA_WIDTH, seq), lambda b, i: (0, b)),
        ],
        out_specs=pl.BlockSpec((tq, MLA_WIDTH), lambda b, i: (b * nq + i, 0)),
        out_shape=jax.ShapeDtypeStruct((t, MLA_WIDTH), BF16),
        scratch_shapes=[
            pltpu.VMEM((MLA_HEADS, nq, tq, tq), F32),
            pltpu.VMEM((MLA_HEADS, 8, tq), F32),
            pltpu.VMEM((MLA_HEADS, V_HEAD + ONES_ROWS, tq), F32),
        ],
        compiler_params=pltpu.CompilerParams(
            dimension_semantics=("parallel", "arbitrary"), vmem_limit_bytes=VMEM_LIMIT),
        name="mla_attn",
    )(q, kn, kr, vt)


def _ssd_kernel(z_ref, xbc_ref, prev_ref, dt_ref, cw_ref, cb_ref, dtb_ref, alog_ref, dexp_ref, ng_ref,
                o_ref, st_ref, xc_ref, dte_ref, acse_ref, wb_ref, *, ts):
    s_idx = pl.program_id(1)
    halo = prev_ref.shape[0]

    @pl.when(s_idx == 0)
    def _():
        st_ref[...] = jnp.zeros_like(st_ref)

    sub = lax.broadcasted_iota(jnp.int32, (halo, SSM_CONV_CH), 0)
    for k in range(SSM_CONV):
        wb_ref[k] = jnp.broadcast_to(cw_ref[k:k + 1, :], (halo, SSM_CONV_CH))
    wb_ref[SSM_CONV] = jnp.broadcast_to(cb_ref[...], (halo, SSM_CONV_CH))

    def conv_tile(i, prev):
        r = pl.multiple_of(i * halo, halo)
        cur = xbc_ref[pl.ds(r, halo), :]
        conv = wb_ref[SSM_CONV] + wb_ref[SSM_CONV - 1] * cur
        for d in range(1, SSM_CONV):
            shifted = jnp.where(sub < d, pltpu.roll(prev, d, axis=0), pltpu.roll(cur, d, axis=0))
            conv = conv + wb_ref[SSM_CONV - 1 - d] * shifted
        xc_ref[pl.ds(r, halo), :] = conv * _sigmoid(conv)
        return cur

    lax.fori_loop(0, ts // halo, conv_tile, jnp.where(s_idx > 0, prev_ref[...], 0.0), unroll=4)

    raw = dt_ref[...] + dtb_ref[...]
    dt = jnp.maximum(raw, 0.0) + jnp.log1p(jnp.exp(-jnp.abs(raw)))
    a = dt * (-jnp.exp(alog_ref[...]))
    lane = lax.broadcasted_iota(jnp.int32, (ts, LANE), 1)
    ri = lax.broadcasted_iota(jnp.int32, (ts, ts), 0)
    ci = lax.broadcasted_iota(jnp.int32, (ts, ts), 1)
    tri = jnp.where((ri // CHUNK == ci // CHUNK) & (ci <= ri), 1.0, 0.0).astype(BF16)
    c3 = _dot(tri, _split3(a, lane))
    acs = c3 + pltpu.roll(c3, LANE - SSM_HEADS, axis=1) + pltpu.roll(c3, LANE - 2 * SSM_HEADS, axis=1)
    er = lax.broadcasted_iota(jnp.int32, (LANE, SSM_INNER), 0)
    ec = lax.broadcasted_iota(jnp.int32, (LANE, SSM_INNER), 1) // SSM_HEAD_DIM
    expand = jnp.where((er % SSM_HEADS == ec) & (er < 3 * SSM_HEADS), 1.0, 0.0).astype(BF16)
    log2e = float(np.log2(np.e))
    both = _dot(jnp.concatenate([_split3(dt, lane), _split3(acs * log2e, lane)], axis=0), expand)
    dte_ref[...] = both[:ts]
    acse_ref[...] = both[ts:]

    li = lax.broadcasted_iota(jnp.int32, (CHUNK, SSM_INNER), 0)
    si = lax.broadcasted_iota(jnp.int32, (CHUNK, SSM_INNER), 1) % SSM_HEAD_DIM
    quad = 4 * SSM_HEAD_DIM
    bd_r = lax.broadcasted_iota(jnp.int32, (quad, quad), 0) // SSM_HEAD_DIM
    bd_c = lax.broadcasted_iota(jnp.int32, (quad, quad), 1) // SSM_HEAD_DIM
    bd_mask = bd_r == bd_c
    dexp = dexp_ref[...]
    ng = ng_ref[...]

    def chunk_body(c, carry):
        r0 = pl.multiple_of(c * CHUNK, CHUNK)
        rows = pl.ds(r0, CHUNK)
        xs = xc_ref[rows, 0:SSM_INNER]
        acx = acse_ref[rows, :]
        last = acx[CHUNK - 1:CHUNK, :]
        xdt = xs * dte_ref[rows, :]
        rv = jnp.sum(jnp.where(li == si, acx, 0.0), axis=0, keepdims=True)
        lmat = jnp.exp2(jnp.where(li >= si, acx - rv, -jnp.inf))
        xdec = (xdt * jnp.exp2(last - acx)).astype(BF16)
        xdt_b = xdt.astype(BF16)
        eacx = jnp.exp2(acx)
        elast = jnp.exp2(last)
        ys = []
        for g in range(SSM_GROUPS):
            gl = slice(g * GROUP_WIDTH, (g + 1) * GROUP_WIDTH)
            b_g = xc_ref[rows, SSM_INNER + g * SSM_STATE:SSM_INNER + (g + 1) * SSM_STATE].astype(BF16)
            c0 = SSM_INNER + SSM_GROUPS * SSM_STATE + g * SSM_STATE
            c_g = xc_ref[rows, c0:c0 + SSM_STATE].astype(BF16)
            gt = _dot_nt(c_g, jnp.concatenate([b_g] * 4, axis=0))
            st_g = st_ref[:, gl]
            y_off = _dot(c_g, st_g.astype(BF16)) * eacx[:, gl]
            yd = []
            for qq in range(GROUP_WIDTH // quad):
                sl = slice(g * GROUP_WIDTH + qq * quad, g * GROUP_WIDTH + (qq + 1) * quad)
                m_q = (gt * lmat[:, sl]).astype(BF16)
                x_q = jnp.concatenate([xdt_b[:, sl]] * 4, axis=0)
                x_q = jnp.where(bd_mask, x_q, jnp.zeros_like(x_q))
                yd.append(_dot(m_q, x_q))
            ys.append(jnp.concatenate(yd, axis=-1) + y_off)
            st_ref[:, gl] = st_g * elast[:, gl] + _dot_tn(b_g, xdec[:, gl])
        y = jnp.concatenate(ys, axis=-1) + xs * dexp
        z = z_ref[rows, :]
        y = y * (z * _sigmoid(z))
        outs = []
        for g in range(SSM_GROUPS):
            yg = y[:, g * GROUP_WIDTH:(g + 1) * GROUP_WIDTH]
            outs.append(yg * lax.rsqrt(jnp.mean(yg * yg, axis=-1, keepdims=True) + EPS))
        o_ref[rows, :] = (jnp.concatenate(outs, axis=-1) * ng).astype(BF16)
        return carry

    lax.fori_loop(0, ts // CHUNK, chunk_body, 0, unroll=True)


def _ssd(u, cw, cb, dtb, alog, dexp, ng, batch, seq, ts=256, halo=8):
    t = u.shape[0]
    ns = seq // ts
    const = lambda b, s: (0, 0)
    rowblk = lambda b, s: b * ns + s
    return pl.pallas_call(
        functools.partial(_ssd_kernel, ts=ts),
        grid=(batch, ns),
        in_specs=[
            pl.BlockSpec((ts, SSM_INNER), lambda b, s: (rowblk(b, s), U_Z // SSM_INNER)),
            pl.BlockSpec((ts, SSM_CONV_CH), lambda b, s: (rowblk(b, s), U_XBC // SSM_CONV_CH)),
            pl.BlockSpec((halo, SSM_CONV_CH),
                         lambda b, s: (jnp.maximum(rowblk(b, s) * (ts // halo) - 1, 0), U_XBC // SSM_CONV_CH)),
            pl.BlockSpec((ts, LANE), lambda b, s: (rowblk(b, s), U_DT // LANE)),
            pl.BlockSpec((SSM_CONV, SSM_CONV_CH), const),
            pl.BlockSpec((1, SSM_CONV_CH), const),
            pl.BlockSpec((1, LANE), const),
            pl.BlockSpec((1, LANE), const),
            pl.BlockSpec((1, SSM_INNER), const),
            pl.BlockSpec((1, SSM_INNER), const),
        ],
        out_specs=pl.BlockSpec((ts, SSM_INNER), lambda b, s: (rowblk(b, s), 0)),
        out_shape=jax.ShapeDtypeStruct((t, SSM_INNER), BF16),
        scratch_shapes=[
            pltpu.VMEM((SSM_STATE, SSM_INNER), F32),
            pltpu.VMEM((ts, SSM_CONV_CH), F32),
            pltpu.VMEM((ts, SSM_INNER), F32),
            pltpu.VMEM((ts, SSM_INNER), F32),
            pltpu.VMEM((SSM_CONV + 1, halo, SSM_CONV_CH), F32),
        ],
        compiler_params=pltpu.CompilerParams(
            dimension_semantics=("parallel", "arbitrary"), vmem_limit_bytes=VMEM_LIMIT),
        name="ssd",
    )(u, u, u, u, cw, cb, dtb, alog, dexp, ng)


def _out_proj_kernel(a_ref, b_ref, x_ref, wa_ref, wb_ref, g1_ref, g2_ref, x1_ref, h2_ref, *, piece):
    for r in range(0, a_ref.shape[0], piece):
        rows = slice(r, r + piece)
        mix = _dot(a_ref[rows, :], wa_ref[...]) + _dot(b_ref[rows, :], wb_ref[...])
        x1 = x_ref[rows, :] + _rms(mix, g1_ref[...])
        x1_ref[rows, :] = x1
        h2_ref[rows, :] = _rms(x1, g2_ref[...]).astype(BF16)


def _out_proj(a_out, b_out, x2, wa, wb, g1, g2, tm=512, piece=256):
    t = x2.shape[0]
    row = lambda i: (i, 0)
    const = lambda i: (0, 0)
    return pl.pallas_call(
        functools.partial(_out_proj_kernel, piece=piece),
        grid=(t // tm,),
        in_specs=[
            pl.BlockSpec((tm, MLA_WIDTH), row),
            pl.BlockSpec((tm, SSM_INNER), row),
            pl.BlockSpec((tm, D_MODEL), row),
            pl.BlockSpec((MLA_WIDTH, D_MODEL), const),
            pl.BlockSpec((SSM_INNER, D_MODEL), const),
            pl.BlockSpec((1, D_MODEL), const),
            pl.BlockSpec((1, D_MODEL), const),
        ],
        out_specs=[pl.BlockSpec((tm, D_MODEL), row), pl.BlockSpec((tm, D_MODEL), row)],
        out_shape=[jax.ShapeDtypeStruct((t, D_MODEL), F32), jax.ShapeDtypeStruct((t, D_MODEL), BF16)],
        compiler_params=pltpu.CompilerParams(
            dimension_semantics=("parallel",), vmem_limit_bytes=VMEM_LIMIT),
        name="out_proj",
    )(a_out, b_out, x2, wa, wb, g1, g2)


def _ffn_kernel(h_ref, x1_ref, wg_ref, wu_ref, wd_ref, cw_ref, cb_ref, g_ref, o_ref,
                acc_ref, act_ref, tail_ref, *, tm, seq):
    s = pl.program_id(0)
    last = pl.num_programs(0) - 1
    nf = tail_ref.shape[0]
    sub = tail_ref.shape[1]
    sp = jnp.minimum(s, last - 1)
    i = sp // nf
    j = sp % nf
    jc = jnp.maximum(s - 1, 0) % nf

    def produce(slot):
        prev = tail_ref[j]
        row = lax.broadcasted_iota(jnp.int32, prev.shape, 0)
        c0 = float(np.sqrt(2.0 / np.pi))
        for r in range(0, tm, FFN_ROWS):
            rows = slice(r, r + FFN_ROWS)
            gate = _dot(h_ref[rows, :], wg_ref[...])
            up = _dot(h_ref[rows, :], wu_ref[...])
            pos = (i * tm + r) % seq + lax.broadcasted_iota(jnp.int32, (FFN_ROWS, 1), 0)
            conv = cb_ref[...] + cw_ref[FFN_CONV - 1:FFN_CONV, :] * gate
            for k in range(FFN_CONV - 1):
                d = FFN_CONV - 1 - k
                rolled = pltpu.roll(gate, d, axis=0)
                head = jnp.where(row < d, pltpu.roll(prev, d, axis=0), rolled[:sub, :])
                shifted = jnp.concatenate([head, rolled[sub:, :]], axis=0)
                conv = conv + cw_ref[k:k + 1, :] * jnp.where(pos >= d, shifted, 0.0)
            act = 0.5 * conv * (1.0 + jnp.tanh(c0 * (conv + 0.044715 * (conv * conv * conv))))
            act_ref[slot, rows, :] = (act * up).astype(BF16)
            prev = gate[FFN_ROWS - sub:, :]
        tail_ref[j] = prev

    def consume(slot):
        acc_ref[...] += _dot(act_ref[slot], wd_ref[...])

    @pl.when(s == 0)
    def _():
        tail_ref[...] = jnp.zeros_like(tail_ref)
        produce(0)

    @pl.when((s > 0) & (jc == 0))
    def _():
        acc_ref[...] = jnp.zeros_like(acc_ref)

    for parity in range(2):
        @pl.when((s > 0) & (s < last) & (s % 2 == parity))
        def _(parity=parity):
            produce(parity)
            consume(1 - parity)

    @pl.when(s == last)
    def _():
        consume((last - 1) % 2)

    @pl.when((s > 0) & (jc == nf - 1))
    def _():
        o_ref[...] = x1_ref[...] + _rms(acc_ref[...], g_ref[...])


def _ffn(h2, x1, wg, wu, wd, cw, cb, g, seq, tm=512, tf=512):
    t = h2.shape[0]
    nf = D_FF // tf
    pairs = (t // tm) * nf
    produced = lambda s: jnp.minimum(s, pairs - 1)
    consumed = lambda s: jnp.maximum(s - 1, 0)
    row_p = lambda s: (produced(s) // nf, 0)
    col_p = lambda s: (0, produced(s) % nf)
    row_c = lambda s: (consumed(s) // nf, 0)
    return pl.pallas_call(
        functools.partial(_ffn_kernel, tm=tm, seq=seq),
        grid=(pairs + 1,),
        in_specs=[
            pl.BlockSpec((tm, D_MODEL), row_p),
            pl.BlockSpec((tm, D_MODEL), row_c),
            pl.BlockSpec((D_MODEL, tf), col_p),
            pl.BlockSpec((D_MODEL, tf), col_p),
            pl.BlockSpec((tf, D_MODEL), lambda s: (consumed(s) % nf, 0)),
            pl.BlockSpec((FFN_CONV, tf), col_p),
            pl.BlockSpec((1, tf), col_p),
            pl.BlockSpec((1, D_MODEL), lambda s: (0, 0)),
        ],
        out_specs=pl.BlockSpec((tm, D_MODEL), row_c),
        out_shape=jax.ShapeDtypeStruct((t, D_MODEL), F32),
        scratch_shapes=[
            pltpu.VMEM((tm, D_MODEL), F32),
            pltpu.VMEM((2, tm, tf), BF16),
            pltpu.VMEM((nf, 8, tf), F32),
        ],
        compiler_params=pltpu.CompilerParams(
            dimension_semantics=("arbitrary",), vmem_limit_bytes=VMEM_LIMIT),
        name="ffn",
    )(h2, x1, wg, wu, wd, cw, cb, g)


def _swap_half(w):
    half = w.shape[-1] // 2
    return jnp.concatenate([-w[..., half:], w[..., :half]], axis=-1)


def _rope_tables(seq):
    inv = 1.0 / (ROPE_THETA ** (jnp.arange(0, QK_ROPE, 2, dtype=F32) / QK_ROPE))
    ang = jnp.arange(seq, dtype=F32)[:, None] * inv[None, :]
    zeros = jnp.zeros((seq, LANE - QK_ROPE), F32)
    cosp = jnp.concatenate([jnp.cos(ang), jnp.cos(ang), zeros], axis=-1)
    sinp = jnp.concatenate([jnp.sin(ang), jnp.sin(ang), zeros], axis=-1)
    return cosp, sinp


def _layer(x2, batch, seq, mix_pre_g, w_in, q_norm_g, w_uq, kv_norm_g, w_ukv, ssm_conv_w, ssm_conv_b,
           dt_bias, a_log, d_skip, ssm_norm_g, w_out, mix_post_g, ffn_pre_g, w_gate, w_up,
           ffn_conv_w, ffn_conv_b, w_down, ffn_post_g):
    row = lambda v: v.reshape(1, -1).astype(F32)
    pad_lane = lambda v: jnp.pad(v.astype(F32), (0, LANE - v.shape[0])).reshape(1, LANE)

    cuts = np.cumsum([Q_LORA, KV_LORA, QK_ROPE, SSM_INNER, SSM_CONV_CH]).tolist()
    w_cq, w_ckv, w_kr, w_z, w_xbc, w_dt = jnp.split(w_in.astype(BF16), cuts, axis=-1)
    w_in_r = jnp.concatenate(
        [w_cq, w_kr, _swap_half(w_kr), w_dt, jnp.zeros((D_MODEL, LANE - SSM_HEADS), BF16),
         w_ckv, w_xbc, w_z], axis=-1)

    wq3 = w_uq.reshape(Q_LORA, MLA_HEADS, QK_NOPE + QK_ROPE)
    wq_rope = wq3[..., QK_NOPE:]
    wq_r = jnp.concatenate([wq3[..., :QK_NOPE], wq_rope, _swap_half(wq_rope)], axis=-1)
    wq_r = wq_r.reshape(Q_LORA, MLA_HEADS * Q_HEAD_PAD).astype(BF16)
    wkv3 = w_ukv.reshape(KV_LORA, MLA_HEADS, QK_NOPE + V_HEAD)
    wk_r = wkv3[..., :QK_NOPE].reshape(KV_LORA, MLA_WIDTH).astype(BF16)
    wvt_r = wkv3[..., QK_NOPE:].reshape(KV_LORA, MLA_WIDTH).T.astype(BF16)

    u = _in_proj(x2, row(mix_pre_g), w_in_r)
    cosp, sinp = _rope_tables(seq)
    q, kn, kr, vt = _mla_proj(u, cosp, sinp, row(q_norm_g), row(kv_norm_g), wq_r, wk_r, wvt_r, seq)
    a_out = _mla_attn(q, kn, kr, vt, batch, seq)
    b_out = _ssd(u, ssm_conv_w.astype(F32), row(ssm_conv_b), pad_lane(dt_bias), pad_lane(a_log),
                 row(jnp.repeat(d_skip, SSM_HEAD_DIM)), row(ssm_norm_g), batch, seq)
    w_out_b = w_out.astype(BF16)
    x1, h2 = _out_proj(a_out, b_out, x2, w_out_b[:MLA_WIDTH], w_out_b[MLA_WIDTH:],
                       row(mix_post_g), row(ffn_pre_g))
    return _ffn(h2, x1, w_gate.astype(BF16), w_up.astype(BF16), w_down.astype(BF16),
                ffn_conv_w.astype(F32), row(ffn_conv_b), row(ffn_post_g), seq)


def kernel(x, mix_pre_g, w_in, q_norm_g, w_uq, kv_norm_g, w_ukv, ssm_conv_w, ssm_conv_b, dt_bias, a_log,
           d_skip, ssm_norm_g, w_out, mix_post_g, ffn_pre_g, w_gate, w_up, ffn_conv_w, ffn_conv_b,
           w_down, ffn_post_g):
    batch, seq, _ = x.shape
    x2 = x.reshape(batch * seq, D_MODEL)
    for l in range(mix_pre_g.shape[0]):
        x2 = _layer(x2, batch, seq, mix_pre_g[l], w_in[l], q_norm_g[l], w_uq[l], kv_norm_g[l], w_ukv[l],
                    ssm_conv_w[l], ssm_conv_b[l], dt_bias[l], a_log[l], d_skip[l], ssm_norm_g[l],
                    w_out[l], mix_post_g[l], ffn_pre_g[l], w_gate[l], w_up[l], ffn_conv_w[l],
                    ffn_conv_b[l], w_down[l], ffn_post_g[l])
    return x2.reshape(batch, seq, D_MODEL)
```

```python
import functools

import jax
import jax.numpy as jnp
import numpy as np
from jax import lax
from jax.experimental import pallas as pl
from jax.experimental.pallas import tpu as pltpu

F32 = jnp.float32
BF16 = jnp.bfloat16

D_MODEL = 2048
CHUNK = 64
EPS = 1e-6

MLA_HEADS = 8
Q_LORA = 768
KV_LORA = 512
QK_NOPE = 128
QK_ROPE = 64
V_HEAD = 128
ROPE_THETA = 10000.0
MLA_WIDTH = MLA_HEADS * V_HEAD
Q_HEAD_PAD = 256

SSM_HEADS = 16
SSM_HEAD_DIM = 64
SSM_INNER = SSM_HEADS * SSM_HEAD_DIM
SSM_GROUPS = 2
SSM_STATE = 128
SSM_CONV = 4
SSM_CONV_CH = SSM_INNER + 2 * SSM_GROUPS * SSM_STATE
GROUP_WIDTH = SSM_INNER // SSM_GROUPS

D_FF = 5632
FFN_CONV = 3
FFN_ROWS = 256

LANE = 128
U_COLS = 4096
U_CQ = 0
U_KR = Q_LORA
U_DT = U_KR + LANE
U_CKV = U_DT + LANE
U_XBC = U_CKV + KV_LORA
U_Z = U_XBC + SSM_CONV_CH

VMEM_LIMIT = 56 * 1024 * 1024


def _rms(x, g):
    return x * lax.rsqrt(jnp.mean(x * x, axis=-1, keepdims=True) + EPS) * g


def _sigmoid(x):
    return 1.0 / (1.0 + jnp.exp(-x))


def _dot(a, b):
    return jnp.dot(a, b, preferred_element_type=F32)


def _dot_nt(a, b):
    return lax.dot_general(a, b, (((1,), (1,)), ((), ())), preferred_element_type=F32)


def _dot_tn(a, b):
    return lax.dot_general(a, b, (((0,), (0,)), ((), ())), preferred_element_type=F32)


def _split3(x, lane):
    xm = jnp.where(lane < SSM_HEADS, x, 0.0)
    hi = xm.astype(BF16).astype(F32)
    r1 = xm - hi
    mid = r1.astype(BF16).astype(F32)
    lo = (r1 - mid).astype(BF16).astype(F32)
    packed = hi + pltpu.roll(mid, SSM_HEADS, axis=1) + pltpu.roll(lo, 2 * SSM_HEADS, axis=1)
    return packed.astype(BF16)


def _in_proj_kernel(x_ref, g_ref, w_ref, o_ref):
    x = x_ref[...]
    r = lax.rsqrt(jnp.mean(x * x, axis=-1, keepdims=True) + EPS)
    o_ref[...] = _dot((x * g_ref[...]).astype(BF16), w_ref[...]) * r


def _in_proj(x2, g, w, tm=512):
    t = x2.shape[0]
    return pl.pallas_call(
        _in_proj_kernel,
        grid=(t // tm,),
        in_specs=[
            pl.BlockSpec((tm, D_MODEL), lambda i: (i, 0)),
            pl.BlockSpec((1, D_MODEL), lambda i: (0, 0)),
            pl.BlockSpec((D_MODEL, U_COLS), lambda i: (0, 0), pipeline_mode=pl.Buffered(1)),
        ],
        out_specs=pl.BlockSpec((tm, U_COLS), lambda i: (i, 0)),
        out_shape=jax.ShapeDtypeStruct((t, U_COLS), F32),
        compiler_params=pltpu.CompilerParams(
            dimension_semantics=("parallel",), vmem_limit_bytes=VMEM_LIMIT),
        name="in_proj",
    )(x2, g, w)


def _rope_pair(blk, cosp, sinp):
    return blk * cosp + pltpu.roll(blk, QK_ROPE, axis=1) * sinp


def _mla_proj_kernel(cq_ref, ckv_ref, kr_ref, cos_ref, sin_ref, qg_ref, kvg_ref, wq_ref, wk_ref, wvt_ref,
                     q_ref, kn_ref, krot_ref, vt_ref, *, scale):
    cosp = cos_ref[...]
    sinp = sin_ref[...]
    qall = _dot(_rms(cq_ref[...], qg_ref[...]).astype(BF16), wq_ref[...])
    for h in range(MLA_HEADS):
        base = h * Q_HEAD_PAD
        q_ref[:, base:base + QK_NOPE] = (qall[:, base:base + QK_NOPE] * scale).astype(BF16)
        rot = _rope_pair(qall[:, base + QK_NOPE:base + Q_HEAD_PAD], cosp, sinp)
        q_ref[:, base + QK_NOPE:base + Q_HEAD_PAD] = (rot * scale).astype(BF16)
    ckv = _rms(ckv_ref[...], kvg_ref[...]).astype(BF16)
    kn_ref[...] = _dot(ckv, wk_ref[...]).astype(BF16)
    vt_ref[...] = _dot_nt(wvt_ref[...], ckv).astype(BF16)
    krot_ref[...] = _rope_pair(kr_ref[...], cosp, sinp).astype(BF16)


def _mla_proj(u, cosp, sinp, qg, kvg, wq, wk, wvt, seq, tm=512):
    t = u.shape[0]
    nseq = seq // tm
    scale = float((QK_NOPE + QK_ROPE) ** -0.5 * np.log2(np.e))
    row = lambda i: (i, 0)
    const = lambda i: (0, 0)
    return pl.pallas_call(
        functools.partial(_mla_proj_kernel, scale=scale),
        grid=(t // tm,),
        in_specs=[
            pl.BlockSpec((tm, Q_LORA), lambda i: (i, U_CQ // Q_LORA)),
            pl.BlockSpec((tm, KV_LORA), lambda i: (i, U_CKV // KV_LORA)),
            pl.BlockSpec((tm, LANE), lambda i: (i, U_KR // LANE)),
            pl.BlockSpec((tm, LANE), lambda i: (i % nseq, 0)),
            pl.BlockSpec((tm, LANE), lambda i: (i % nseq, 0)),
            pl.BlockSpec((1, Q_LORA), const),
            pl.BlockSpec((1, KV_LORA), const),
            pl.BlockSpec((Q_LORA, MLA_HEADS * Q_HEAD_PAD), const),
            pl.BlockSpec((KV_LORA, MLA_WIDTH), const),
            pl.BlockSpec((MLA_WIDTH, KV_LORA), const),
        ],
        out_specs=[
            pl.BlockSpec((tm, MLA_HEADS * Q_HEAD_PAD), row),
            pl.BlockSpec((tm, MLA_WIDTH), row),
            pl.BlockSpec((tm, LANE), row),
            pl.BlockSpec((MLA_WIDTH, tm), lambda i: (0, i)),
        ],
        out_shape=[
            jax.ShapeDtypeStruct((t, MLA_HEADS * Q_HEAD_PAD), BF16),
            jax.ShapeDtypeStruct((t, MLA_WIDTH), BF16),
            jax.ShapeDtypeStruct((t, LANE), BF16),
            jax.ShapeDtypeStruct((MLA_WIDTH, t), BF16),
        ],
        compiler_params=pltpu.CompilerParams(
            dimension_semantics=("parallel",), vmem_limit_bytes=VMEM_LIMIT),
        name="mla_proj",
    )(u, u, u, cosp, sinp, qg, kvg, wq, wk, wvt)


ONES_ROWS = 16


def _attn_kernel(q_ref, kn_ref, kr_ref, vt_ref, o_ref, s_ref, mrun_ref, acc_ref, *, tq):
    qi = pl.program_id(1)
    sublanes = mrun_ref.shape[1]
    krow = lax.broadcasted_iota(jnp.int32, (tq, tq), 0) // CHUNK
    qcol = lax.broadcasted_iota(jnp.int32, (tq, tq), 1) // CHUNK
    diag_mask = krow <= qcol
    ones = jnp.ones((ONES_ROWS, tq), BF16)

    mrun_ref[...] = jnp.full(mrun_ref.shape, -1e30, F32)
    acc_ref[...] = jnp.zeros_like(acc_ref)

    def scores(j, masked):
        ks = pl.multiple_of(j * tq, tq)
        kr = kr_ref[pl.ds(ks, tq), :]
        for h in range(MLA_HEADS):
            q = q_ref[:, h * Q_HEAD_PAD:(h + 1) * Q_HEAD_PAD]
            k = jnp.concatenate([kn_ref[pl.ds(ks, tq), h * QK_NOPE:(h + 1) * QK_NOPE], kr], axis=-1)
            s = _dot_nt(k, q)
            if masked:
                s = jnp.where(diag_mask, s, -1e30)
            s_ref[h, j] = s
            tile_max = jnp.max(s.reshape(tq // sublanes, sublanes, tq), axis=0)
            mrun_ref[h] = jnp.maximum(mrun_ref[h], tile_max)

    def scores_body(j, carry):
        scores(j, masked=False)
        return carry

    lax.fori_loop(0, qi, scores_body, 0)
    scores(qi, masked=True)

    m = [jnp.max(mrun_ref[h], axis=0, keepdims=True) for h in range(MLA_HEADS)]

    def accumulate(j, carry):
        ks = pl.multiple_of(j * tq, tq)
        for h in range(MLA_HEADS):
            p = jnp.exp2(s_ref[h, j] - m[h]).astype(BF16)
            vt = jnp.concatenate([vt_ref[h * V_HEAD:(h + 1) * V_HEAD, pl.ds(ks, tq)], ones], axis=0)
            acc_ref[h] += _dot(vt, p)
        return carry

    lax.fori_loop(0, qi + 1, accumulate, 0)
    for h in range(MLA_HEADS):
        acc = acc_ref[h]
        out_t = acc[:V_HEAD, :] / acc[V_HEAD:V_HEAD + 1, :]
        o_ref[:, h * V_HEAD:(h + 1) * V_HEAD] = out_t.T.astype(BF16)


def _mla_attn(q, kn, kr, vt, batch, seq, tq=256):
    t = q.shape[0]
    nq = seq // tq
    return pl.pallas_call(
        functools.partial(_attn_kernel, tq=tq),
        grid=(batch, nq),
        in_specs=[
            pl.BlockSpec((tq, MLA_HEADS * Q_HEAD_PAD), lambda b, i: (b * nq + i, 0)),
            pl.BlockSpec((seq, MLA_WIDTH), lambda b, i: (b, 0)),
            pl.BlockSpec((seq, LANE), lambda b, i: (b, 0)),
            pl.BlockSpec((MLA_WIDTH, seq), lambda b, i: (0, b)),
        ],
        out_specs=pl.BlockSpec((tq, MLA_WIDTH), lambda b, i: (b * nq + i, 0)),
        out_shape=jax.ShapeDtypeStruct((t, MLA_WIDTH), BF16),
        scratch_shapes=[
            pltpu.VMEM((MLA_HEADS, nq, tq, tq), F32),
            pltpu.VMEM((MLA_HEADS, 8, tq), F32),
            pltpu.VMEM((MLA_HEADS, V_HEAD + ONES_ROWS, tq), F32),
        ],
        compiler_params=pltpu.CompilerParams(
            dimension_semantics=("parallel", "arbitrary"), vmem_limit_bytes=VMEM_LIMIT),
        name="mla_attn",
    )(q, kn, kr, vt)


def _ssd_kernel(z_ref, xbc_ref, prev_ref, dt_ref, cw_ref, cb_ref, dtb_ref, alog_ref, dexp_ref, ng_ref,
                o_ref, st_ref, xc_ref, dte_ref, acse_ref, wb_ref, *, ts):
    s_idx = pl.program_id(1)
    halo = prev_ref.shape[0]

    @pl.when(s_idx == 0)
    def _():
        st_ref[...] = jnp.zeros_like(st_ref)

    sub = lax.broadcasted_iota(jnp.int32, (halo, SSM_CONV_CH), 0)
    for k in range(SSM_CONV):
        wb_ref[k] = jnp.broadcast_to(cw_ref[k:k + 1, :], (halo, SSM_CONV_CH))
    wb_ref[SSM_CONV] = jnp.broadcast_to(cb_ref[...], (halo, SSM_CONV_CH))

    def conv_tile(i, prev):
        r = pl.multiple_of(i * halo, halo)
        cur = xbc_ref[pl.ds(r, halo), :]
        conv = wb_ref[SSM_CONV] + wb_ref[SSM_CONV - 1] * cur
        for d in range(1, SSM_CONV):
            shifted = jnp.where(sub < d, pltpu.roll(prev, d, axis=0), pltpu.roll(cur, d, axis=0))
            conv = conv + wb_ref[SSM_CONV - 1 - d] * shifted
        xc_ref[pl.ds(r, halo), :] = conv * _sigmoid(conv)
        return cur

    lax.fori_loop(0, ts // halo, conv_tile, jnp.where(s_idx > 0, prev_ref[...], 0.0), unroll=True)

    raw = dt_ref[...] + dtb_ref[...]
    dt = jnp.maximum(raw, 0.0) + jnp.log1p(jnp.exp(-jnp.abs(raw)))
    a = dt * (-jnp.exp(alog_ref[...]))
    lane = lax.broadcasted_iota(jnp.int32, (ts, LANE), 1)
    ri = lax.broadcasted_iota(jnp.int32, (ts, ts), 0)
    ci = lax.broadcasted_iota(jnp.int32, (ts, ts), 1)
    tri = jnp.where((ri // CHUNK == ci // CHUNK) & (ci <= ri), 1.0, 0.0).astype(BF16)
    c3 = _dot(tri, _split3(a, lane))
    acs = c3 + pltpu.roll(c3, LANE - SSM_HEADS, axis=1) + pltpu.roll(c3, LANE - 2 * SSM_HEADS, axis=1)
    er = lax.broadcasted_iota(jnp.int32, (LANE, SSM_INNER), 0)
    ec = lax.broadcasted_iota(jnp.int32, (LANE, SSM_INNER), 1) // SSM_HEAD_DIM
    expand = jnp.where((er % SSM_HEADS == ec) & (er < 3 * SSM_HEADS), 1.0, 0.0).astype(BF16)
    log2e = float(np.log2(np.e))
    both = _dot(jnp.concatenate([_split3(dt, lane), _split3(acs * log2e, lane)], axis=0), expand)
    dte_ref[...] = both[:ts]
    acse_ref[...] = both[ts:]

    li = lax.broadcasted_iota(jnp.int32, (CHUNK, SSM_INNER), 0)
    si = lax.broadcasted_iota(jnp.int32, (CHUNK, SSM_INNER), 1) % SSM_HEAD_DIM
    quad = 4 * SSM_HEAD_DIM
    bd_r = lax.broadcasted_iota(jnp.int32, (quad, quad), 0) // SSM_HEAD_DIM
    bd_c = lax.broadcasted_iota(jnp.int32, (quad, quad), 1) // SSM_HEAD_DIM
    bd_mask = bd_r == bd_c
    dexp = dexp_ref[...]
    ng = ng_ref[...]

    def chunk_body(c, carry):
        r0 = pl.multiple_of(c * CHUNK, CHUNK)
        rows = pl.ds(r0, CHUNK)
        xs = xc_ref[rows, 0:SSM_INNER]
        acx = acse_ref[rows, :]
        last = acx[CHUNK - 1:CHUNK, :]
        xdt = xs * dte_ref[rows, :]
        rv = jnp.sum(jnp.where(li == si, acx, 0.0), axis=0, keepdims=True)
        lmat = jnp.exp2(jnp.where(li >= si, acx - rv, -jnp.inf))
        xdec = (xdt * jnp.exp2(last - acx)).astype(BF16)
        xdt_b = xdt.astype(BF16)
        eacx = jnp.exp2(acx)
        elast = jnp.exp2(last)
        ys = []
        for g in range(SSM_GROUPS):
            gl = slice(g * GROUP_WIDTH, (g + 1) * GROUP_WIDTH)
            b_g = xc_ref[rows, SSM_INNER + g * SSM_STATE:SSM_INNER + (g + 1) * SSM_STATE].astype(BF16)
            c0 = SSM_INNER + SSM_GROUPS * SSM_STATE + g * SSM_STATE
            c_g = xc_ref[rows, c0:c0 + SSM_STATE].astype(BF16)
            gt = _dot_nt(c_g, jnp.concatenate([b_g] * 4, axis=0))
            st_g = st_ref[:, gl]
            y_off = _dot(c_g, st_g.astype(BF16)) * eacx[:, gl]
            yd = []
            for qq in range(GROUP_WIDTH // quad):
                sl = slice(g * GROUP_WIDTH + qq * quad, g * GROUP_WIDTH + (qq + 1) * quad)
                m_q = (gt * lmat[:, sl]).astype(BF16)
                x_q = jnp.concatenate([xdt_b[:, sl]] * 4, axis=0)
                x_q = jnp.where(bd_mask, x_q, jnp.zeros_like(x_q))
                yd.append(_dot(m_q, x_q))
            ys.append(jnp.concatenate(yd, axis=-1) + y_off)
            st_ref[:, gl] = st_g * elast[:, gl] + _dot_tn(b_g, xdec[:, gl])
        y = jnp.concatenate(ys, axis=-1) + xs * dexp
        z = z_ref[rows, :]
        y = y * (z * _sigmoid(z))
        outs = []
        for g in range(SSM_GROUPS):
            yg = y[:, g * GROUP_WIDTH:(g + 1) * GROUP_WIDTH]
            outs.append(yg * lax.rsqrt(jnp.mean(yg * yg, axis=-1, keepdims=True) + EPS))
        o_ref[rows, :] = (jnp.concatenate(outs, axis=-1) * ng).astype(BF16)
        return carry

    lax.fori_loop(0, ts // CHUNK, chunk_body, 0, unroll=True)


def _ssd(u, cw, cb, dtb, alog, dexp, ng, batch, seq, ts=256, halo=8):
    t = u.shape[0]
    ns = seq // ts
    const = lambda b, s: (0, 0)
    rowblk = lambda b, s: b * ns + s
    return pl.pallas_call(
        functools.partial(_ssd_kernel, ts=ts),
        grid=(batch, ns),
        in_specs=[
            pl.BlockSpec((ts, SSM_INNER), lambda b, s: (rowblk(b, s), U_Z // SSM_INNER)),
            pl.BlockSpec((ts, SSM_CONV_CH), lambda b, s: (rowblk(b, s), U_XBC // SSM_CONV_CH)),
            pl.BlockSpec((halo, SSM_CONV_CH),
                         lambda b, s: (jnp.maximum(rowblk(b, s) * (ts // halo) - 1, 0), U_XBC // SSM_CONV_CH)),
            pl.BlockSpec((ts, LANE), lambda b, s: (rowblk(b, s), U_DT // LANE)),
            pl.BlockSpec((SSM_CONV, SSM_CONV_CH), const),
            pl.BlockSpec((1, SSM_CONV_CH), const),
            pl.BlockSpec((1, LANE), const),
            pl.BlockSpec((1, LANE), const),
            pl.BlockSpec((1, SSM_INNER), const),
            pl.BlockSpec((1, SSM_INNER), const),
        ],
        out_specs=pl.BlockSpec((ts, SSM_INNER), lambda b, s: (rowblk(b, s), 0)),
        out_shape=jax.ShapeDtypeStruct((t, SSM_INNER), BF16),
        scratch_shapes=[
            pltpu.VMEM((SSM_STATE, SSM_INNER), F32),
            pltpu.VMEM((ts, SSM_CONV_CH), F32),
            pltpu.VMEM((ts, SSM_INNER), F32),
            pltpu.VMEM((ts, SSM_INNER), F32),
            pltpu.VMEM((SSM_CONV + 1, halo, SSM_CONV_CH), F32),
        ],
        compiler_params=pltpu.CompilerParams(
            dimension_semantics=("parallel", "arbitrary"), vmem_limit_bytes=VMEM_LIMIT),
        name="ssd",
    )(u, u, u, u, cw, cb, dtb, alog, dexp, ng)


def _out_proj_kernel(a_ref, b_ref, x_ref, wa_ref, wb_ref, g1_ref, g2_ref, x1_ref, h2_ref, *, piece):
    for r in range(0, a_ref.shape[0], piece):
        rows = slice(r, r + piece)
        mix = _dot(a_ref[rows, :], wa_ref[...]) + _dot(b_ref[rows, :], wb_ref[...])
        x1 = x_ref[rows, :] + _rms(mix, g1_ref[...])
        x1_ref[rows, :] = x1
        h2_ref[rows, :] = _rms(x1, g2_ref[...]).astype(BF16)


def _out_proj(a_out, b_out, x2, wa, wb, g1, g2, tm=512, piece=256):
    t = x2.shape[0]
    row = lambda i: (i, 0)
    const = lambda i: (0, 0)
    return pl.pallas_call(
        functools.partial(_out_proj_kernel, piece=piece),
        grid=(t // tm,),
        in_specs=[
            pl.BlockSpec((tm, MLA_WIDTH), row),
            pl.BlockSpec((tm, SSM_INNER), row),
            pl.BlockSpec((tm, D_MODEL), row),
            pl.BlockSpec((MLA_WIDTH, D_MODEL), const),
            pl.BlockSpec((SSM_INNER, D_MODEL), const),
            pl.BlockSpec((1, D_MODEL), const),
            pl.BlockSpec((1, D_MODEL), const),
        ],
        out_specs=[pl.BlockSpec((tm, D_MODEL), row), pl.BlockSpec((tm, D_MODEL), row)],
        out_shape=[jax.ShapeDtypeStruct((t, D_MODEL), F32), jax.ShapeDtypeStruct((t, D_MODEL), BF16)],
        compiler_params=pltpu.CompilerParams(
            dimension_semantics=("parallel",), vmem_limit_bytes=VMEM_LIMIT),
        name="out_proj",
    )(a_out, b_out, x2, wa, wb, g1, g2)


def _ffn_kernel(h_ref, x1_ref, wg_ref, wu_ref, wd_ref, cw_ref, g_ref, o_ref,
                acc_ref, act_ref, tail_ref, *, tm, tf, seq):
    i = pl.program_id(0)
    j = pl.program_id(1)
    nt = pl.num_programs(0) - 1
    nf = tail_ref.shape[0]
    sub = tail_ref.shape[1]
    first = (i == 0) & (j == 0)
    cols = pl.ds(pl.multiple_of(j * tf, tf), tf)

    def produce(slot):
        cw = cw_ref[:, cols]
        prev = tail_ref[j]
        row = lax.broadcasted_iota(jnp.int32, prev.shape, 0)
        c0 = float(np.sqrt(2.0 / np.pi))
        for r in range(0, tm, FFN_ROWS):
            rows = slice(r, r + FFN_ROWS)
            gate = _dot(h_ref[rows, :], wg_ref[...])
            up = _dot(h_ref[rows, :], wu_ref[...])
            pos = (i * tm + r) % seq + lax.broadcasted_iota(jnp.int32, (FFN_ROWS, 1), 0)
            conv = cw[FFN_CONV:FFN_CONV + 1, :] + cw[FFN_CONV - 1:FFN_CONV, :] * gate
            for k in range(FFN_CONV - 1):
                d = FFN_CONV - 1 - k
                rolled = pltpu.roll(gate, d, axis=0)
                head = jnp.where(row < d, pltpu.roll(prev, d, axis=0), rolled[:sub, :])
                shifted = jnp.concatenate([head, rolled[sub:, :]], axis=0)
                conv = conv + cw[k:k + 1, :] * jnp.where(pos >= d, shifted, 0.0)
            act = 0.5 * conv * (1.0 + jnp.tanh(c0 * (conv + 0.044715 * (conv * conv * conv))))
            act_ref[slot, rows, :] = (act * up).astype(BF16)
            prev = gate[FFN_ROWS - sub:, :]
        tail_ref[j] = prev

    def consume(slot):
        acc_ref[...] += _dot(act_ref[slot], wd_ref[...])

    @pl.when(first)
    def _():
        tail_ref[...] = jnp.zeros_like(tail_ref)
        produce(0)

    @pl.when(j == 1)
    def _():
        acc_ref[...] = jnp.zeros_like(acc_ref)

    for parity in range(2):
        @pl.when(jnp.logical_not(first) & (i < nt) & ((i + j) % 2 == parity))
        def _(parity=parity):
            produce(parity)
            consume(1 - parity)

    @pl.when((i == nt) & (j == 0))
    def _():
        consume((nt * nf - 1) % 2)

    @pl.when(jnp.logical_not(first) & (j == 0))
    def _():
        o_ref[...] = x1_ref[...] + _rms(acc_ref[...], g_ref[...])


def _ffn(h2, x1, wg, wu, wd, cwb, g, seq, tm=512, tf=512):
    t = h2.shape[0]
    nt = t // tm
    nf = D_FF // tf
    assert nf % 2 == 1
    row_p = lambda i, j: (jnp.minimum(i, nt - 1), 0)
    col_p = lambda i, j: (0, jnp.where(i < nt, j, nf - 1))
    row_c = lambda i, j: (jnp.where(j > 0, jnp.minimum(i, nt - 1), jnp.maximum(i - 1, 0)), 0)
    return pl.pallas_call(
        functools.partial(_ffn_kernel, tm=tm, tf=tf, seq=seq),
        grid=(nt + 1, nf),
        in_specs=[
            pl.BlockSpec((tm, D_MODEL), row_p),
            pl.BlockSpec((tm, D_MODEL), row_c),
            pl.BlockSpec((D_MODEL, tf), col_p),
            pl.BlockSpec((D_MODEL, tf), col_p),
            pl.BlockSpec((tf, D_MODEL), lambda i, j: (jnp.where(j > 0, j - 1, nf - 1), 0)),
            pl.BlockSpec((FFN_CONV + 1, D_FF), lambda i, j: (0, 0)),
            pl.BlockSpec((1, D_MODEL), lambda i, j: (0, 0)),
        ],
        out_specs=pl.BlockSpec((tm, D_MODEL), row_c),
        out_shape=jax.ShapeDtypeStruct((t, D_MODEL), F32),
        scratch_shapes=[
            pltpu.VMEM((tm, D_MODEL), F32),
            pltpu.VMEM((2, tm, tf), BF16),
            pltpu.VMEM((nf, 8, tf), F32),
        ],
        compiler_params=pltpu.CompilerParams(
            dimension_semantics=("arbitrary", "arbitrary"), vmem_limit_bytes=VMEM_LIMIT),
        name="ffn",
    )(h2, x1, wg, wu, wd, cwb, g)


def _swap_half(w):
    half = w.shape[-1] // 2
    return jnp.concatenate([-w[..., half:], w[..., :half]], axis=-1)


def _rope_tables(seq):
    inv = 1.0 / (ROPE_THETA ** (jnp.arange(0, QK_ROPE, 2, dtype=F32) / QK_ROPE))
    ang = jnp.arange(seq, dtype=F32)[:, None] * inv[None, :]
    zeros = jnp.zeros((seq, LANE - QK_ROPE), F32)
    cosp = jnp.concatenate([jnp.cos(ang), jnp.cos(ang), zeros], axis=-1)
    sinp = jnp.concatenate([jnp.sin(ang), jnp.sin(ang), zeros], axis=-1)
    return cosp, sinp


def _layer(x2, batch, seq, mix_pre_g, w_in, q_norm_g, w_uq, kv_norm_g, w_ukv, ssm_conv_w, ssm_conv_b,
           dt_bias, a_log, d_skip, ssm_norm_g, w_out, mix_post_g, ffn_pre_g, w_gate, w_up,
           ffn_conv_w, ffn_conv_b, w_down, ffn_post_g):
    row = lambda v: v.reshape(1, -1).astype(F32)
    pad_lane = lambda v: jnp.pad(v.astype(F32), (0, LANE - v.shape[0])).reshape(1, LANE)

    cuts = [0] + np.cumsum([Q_LORA, KV_LORA, QK_ROPE, SSM_INNER, SSM_CONV_CH, SSM_HEADS]).tolist()
    w_cq, w_ckv, w_kr, w_z, w_xbc, w_dt = [w_in[:, a:b] for a, b in zip(cuts[:-1], cuts[1:])]
    w_in_r = jnp.concatenate(
        [w_cq, w_kr, _swap_half(w_kr), w_dt, jnp.zeros((D_MODEL, LANE - SSM_HEADS), w_in.dtype),
         w_ckv, w_xbc, w_z], axis=-1).astype(BF16)

    wq3 = w_uq.reshape(Q_LORA, MLA_HEADS, QK_NOPE + QK_ROPE)
    wq_rope = wq3[..., QK_NOPE:]
    wq_r = jnp.concatenate([wq3[..., :QK_NOPE], wq_rope, _swap_half(wq_rope)], axis=-1)
    wq_r = wq_r.reshape(Q_LORA, MLA_HEADS * Q_HEAD_PAD).astype(BF16)
    wkv3 = w_ukv.reshape(KV_LORA, MLA_HEADS, QK_NOPE + V_HEAD)
    wk_r = wkv3[..., :QK_NOPE].reshape(KV_LORA, MLA_WIDTH).astype(BF16)
    wvt_r = wkv3[..., QK_NOPE:].reshape(KV_LORA, MLA_WIDTH).T.astype(BF16)

    u = _in_proj(x2, row(mix_pre_g), w_in_r)
    cosp, sinp = _rope_tables(seq)
    q, kn, kr, vt = _mla_proj(u, cosp, sinp, row(q_norm_g), row(kv_norm_g), wq_r, wk_r, wvt_r, seq)
    a_out = _mla_attn(q, kn, kr, vt, batch, seq)
    b_out = _ssd(u, ssm_conv_w.astype(F32), row(ssm_conv_b), pad_lane(dt_bias), pad_lane(a_log),
                 row(jnp.repeat(d_skip, SSM_HEAD_DIM)), row(ssm_norm_g), batch, seq)
    w_out_b = w_out.astype(BF16)
    x1, h2 = _out_proj(a_out, b_out, x2, w_out_b[:MLA_WIDTH], w_out_b[MLA_WIDTH:],
                       row(mix_post_g), row(ffn_pre_g))
    return _ffn(h2, x1, w_gate.astype(BF16), w_up.astype(BF16), w_down.astype(BF16),
                jnp.concatenate([ffn_conv_w.astype(F32), row(ffn_conv_b)], axis=0), row(ffn_post_g), seq)


def kernel(x, mix_pre_g, w_in, q_norm_g, w_uq, kv_norm_g, w_ukv, ssm_conv_w, ssm_conv_b, dt_bias, a_log,
           d_skip, ssm_norm_g, w_out, mix_post_g, ffn_pre_g, w_gate, w_up, ffn_conv_w, ffn_conv_b,
           w_down, ffn_post_g):
    batch, seq, _ = x.shape
    x2 = x.reshape(batch * seq, D_MODEL)
    for l in range(mix_pre_g.shape[0]):
        x2 = _layer(x2, batch, seq, mix_pre_g[l], w_in[l], q_norm_g[l], w_uq[l], kv_norm_g[l], w_ukv[l],
                    ssm_conv_w[l], ssm_conv_b[l], dt_bias[l], a_log[l], d_skip[l], ssm_norm_g[l],
                    w_out[l], mix_post_g[l], ffn_pre_g[l], w_gate[l], w_up[l], ffn_conv_w[l],
                    ffn_conv_b[l], w_down[l], ffn_post_g[l])
    return x2.reshape(batch, seq, D_MODEL)
```

```python
import functools

import jax
import jax.numpy as jnp
import numpy as np
from jax import lax
from jax.experimental import pallas as pl
from jax.experimental.pallas import tpu as pltpu

F32 = jnp.float32
BF16 = jnp.bfloat16

D_MODEL = 2048
CHUNK = 64
EPS = 1e-6

MLA_HEADS = 8
Q_LORA = 768
KV_LORA = 512
QK_NOPE = 128
QK_ROPE = 64
V_HEAD = 128
ROPE_THETA = 10000.0
MLA_WIDTH = MLA_HEADS * V_HEAD
Q_HEAD_PAD = 256

SSM_HEADS = 16
SSM_HEAD_DIM = 64
SSM_INNER = SSM_HEADS * SSM_HEAD_DIM
SSM_GROUPS = 2
SSM_STATE = 128
SSM_CONV = 4
SSM_CONV_CH = SSM_INNER + 2 * SSM_GROUPS * SSM_STATE
GROUP_WIDTH = SSM_INNER // SSM_GROUPS

D_FF = 5632
FFN_CONV = 3
FFN_ROWS = 256

LANE = 128
U_COLS = 4096
U_CQ = 0
U_KR = Q_LORA
U_DT = U_KR + LANE
U_CKV = U_DT + LANE
U_XBC = U_CKV + KV_LORA
U_Z = U_XBC + SSM_CONV_CH

VMEM_LIMIT = 56 * 1024 * 1024


def _rms(x, g):
    return x * lax.rsqrt(jnp.mean(x * x, axis=-1, keepdims=True) + EPS) * g


def _sigmoid(x):
    return 1.0 / (1.0 + jnp.exp(-x))


def _dot(a, b):
    return jnp.dot(a, b, preferred_element_type=F32)


def _dot_nt(a, b):
    return lax.dot_general(a, b, (((1,), (1,)), ((), ())), preferred_element_type=F32)


def _dot_tn(a, b):
    return lax.dot_general(a, b, (((0,), (0,)), ((), ())), preferred_element_type=F32)


def _split3(x, lane):
    xm = jnp.where(lane < SSM_HEADS, x, 0.0)
    hi = xm.astype(BF16).astype(F32)
    r1 = xm - hi
    mid = r1.astype(BF16).astype(F32)
    lo = (r1 - mid).astype(BF16).astype(F32)
    packed = hi + pltpu.roll(mid, SSM_HEADS, axis=1) + pltpu.roll(lo, 2 * SSM_HEADS, axis=1)
    return packed.astype(BF16)


def _in_proj_kernel(x_ref, g_ref, w_ref, o_ref):
    x = x_ref[...]
    r = lax.rsqrt(jnp.mean(x * x, axis=-1, keepdims=True) + EPS)
    o_ref[...] = _dot((x * g_ref[...]).astype(BF16), w_ref[...]) * r


def _in_proj(x2, g, w, tm=512):
    t = x2.shape[0]
    return pl.pallas_call(
        _in_proj_kernel,
        grid=(t // tm,),
        in_specs=[
            pl.BlockSpec((tm, D_MODEL), lambda i: (i, 0)),
            pl.BlockSpec((1, D_MODEL), lambda i: (0, 0)),
            pl.BlockSpec((D_MODEL, U_COLS), lambda i: (0, 0), pipeline_mode=pl.Buffered(1)),
        ],
        out_specs=pl.BlockSpec((tm, U_COLS), lambda i: (i, 0)),
        out_shape=jax.ShapeDtypeStruct((t, U_COLS), F32),
        compiler_params=pltpu.CompilerParams(
            dimension_semantics=("parallel",), vmem_limit_bytes=VMEM_LIMIT),
        name="in_proj",
    )(x2, g, w)


def _rope_pair(blk, cosp, sinp):
    return blk * cosp + pltpu.roll(blk, QK_ROPE, axis=1) * sinp


def _mla_proj_kernel(cq_ref, ckv_ref, kr_ref, cos_ref, sin_ref, qg_ref, kvg_ref, wq_ref, wk_ref, wvt_ref,
                     q_ref, kn_ref, krot_ref, vt_ref, *, scale):
    cosp = cos_ref[...]
    sinp = sin_ref[...]
    qall = _dot(_rms(cq_ref[...], qg_ref[...]).astype(BF16), wq_ref[...])
    for h in range(MLA_HEADS):
        base = h * Q_HEAD_PAD
        q_ref[:, base:base + QK_NOPE] = (qall[:, base:base + QK_NOPE] * scale).astype(BF16)
        rot = _rope_pair(qall[:, base + QK_NOPE:base + Q_HEAD_PAD], cosp, sinp)
        q_ref[:, base + QK_NOPE:base + Q_HEAD_PAD] = (rot * scale).astype(BF16)
    ckv = _rms(ckv_ref[...], kvg_ref[...]).astype(BF16)
    kn_ref[...] = _dot(ckv, wk_ref[...]).astype(BF16)
    vt_ref[...] = _dot_nt(wvt_ref[...], ckv).astype(BF16)
    krot_ref[...] = _rope_pair(kr_ref[...], cosp, sinp).astype(BF16)


def _mla_proj(u, cosp, sinp, qg, kvg, wq, wk, wvt, seq, tm=512):
    t = u.shape[0]
    nseq = seq // tm
    scale = float((QK_NOPE + QK_ROPE) ** -0.5 * np.log2(np.e))
    row = lambda i: (i, 0)
    const = lambda i: (0, 0)
    return pl.pallas_call(
        functools.partial(_mla_proj_kernel, scale=scale),
        grid=(t // tm,),
        in_specs=[
            pl.BlockSpec((tm, Q_LORA), lambda i: (i, U_CQ // Q_LORA)),
            pl.BlockSpec((tm, KV_LORA), lambda i: (i, U_CKV // KV_LORA)),
            pl.BlockSpec((tm, LANE), lambda i: (i, U_KR // LANE)),
            pl.BlockSpec((tm, LANE), lambda i: (i % nseq, 0)),
            pl.BlockSpec((tm, LANE), lambda i: (i % nseq, 0)),
            pl.BlockSpec((1, Q_LORA), const),
            pl.BlockSpec((1, KV_LORA), const),
            pl.BlockSpec((Q_LORA, MLA_HEADS * Q_HEAD_PAD), const),
            pl.BlockSpec((KV_LORA, MLA_WIDTH), const),
            pl.BlockSpec((MLA_WIDTH, KV_LORA), const),
        ],
        out_specs=[
            pl.BlockSpec((tm, MLA_HEADS * Q_HEAD_PAD), row),
            pl.BlockSpec((tm, MLA_WIDTH), row),
            pl.BlockSpec((tm, LANE), row),
            pl.BlockSpec((MLA_WIDTH, tm), lambda i: (0, i)),
        ],
        out_shape=[
            jax.ShapeDtypeStruct((t, MLA_HEADS * Q_HEAD_PAD), BF16),
            jax.ShapeDtypeStruct((t, MLA_WIDTH), BF16),
            jax.ShapeDtypeStruct((t, LANE), BF16),
            jax.ShapeDtypeStruct((MLA_WIDTH, t), BF16),
        ],
        compiler_params=pltpu.CompilerParams(
            dimension_semantics=("parallel",), vmem_limit_bytes=VMEM_LIMIT),
        name="mla_proj",
    )(u, u, u, cosp, sinp, qg, kvg, wq, wk, wvt)


ONES_ROWS = 16


def _attn_kernel(q_ref, kn_ref, kr_ref, vt_ref, o_ref, s_ref, mrun_ref, acc_ref, *, tq):
    qi = pl.program_id(1)
    sublanes = mrun_ref.shape[1]
    krow = lax.broadcasted_iota(jnp.int32, (tq, tq), 0) // CHUNK
    qcol = lax.broadcasted_iota(jnp.int32, (tq, tq), 1) // CHUNK
    diag_mask = krow <= qcol
    ones = jnp.ones((ONES_ROWS, tq), BF16)

    mrun_ref[...] = jnp.full(mrun_ref.shape, -1e30, F32)
    acc_ref[...] = jnp.zeros_like(acc_ref)

    def scores(j, masked):
        ks = pl.multiple_of(j * tq, tq)
        kr = kr_ref[pl.ds(ks, tq), :]
        for h in range(MLA_HEADS):
            q = q_ref[:, h * Q_HEAD_PAD:(h + 1) * Q_HEAD_PAD]
            k = jnp.concatenate([kn_ref[pl.ds(ks, tq), h * QK_NOPE:(h + 1) * QK_NOPE], kr], axis=-1)
            s = _dot_nt(k, q)
            if masked:
                s = jnp.where(diag_mask, s, -1e30)
            s_ref[h, j] = s
            tile_max = jnp.max(s.reshape(tq // sublanes, sublanes, tq), axis=0)
            mrun_ref[h] = jnp.maximum(mrun_ref[h], tile_max)

    def scores_body(j, carry):
        scores(j, masked=False)
        return carry

    lax.fori_loop(0, qi, scores_body, 0)
    scores(qi, masked=True)

    m = [jnp.max(mrun_ref[h], axis=0, keepdims=True) for h in range(MLA_HEADS)]

    def accumulate(j, carry):
        ks = pl.multiple_of(j * tq, tq)
        for h in range(MLA_HEADS):
            p = jnp.exp2(s_ref[h, j] - m[h]).astype(BF16)
            vt = jnp.concatenate([vt_ref[h * V_HEAD:(h + 1) * V_HEAD, pl.ds(ks, tq)], ones], axis=0)
            acc_ref[h] += _dot(vt, p)
        return carry

    lax.fori_loop(0, qi + 1, accumulate, 0)
    for h in range(MLA_HEADS):
        acc = acc_ref[h]
        out_t = acc[:V_HEAD, :] / acc[V_HEAD:V_HEAD + 1, :]
        o_ref[:, h * V_HEAD:(h + 1) * V_HEAD] = out_t.T.astype(BF16)


def _mla_attn(q, kn, kr, vt, batch, seq, tq=256):
    t = q.shape[0]
    nq = seq // tq
    return pl.pallas_call(
        functools.partial(_attn_kernel, tq=tq),
        grid=(batch, nq),
        in_specs=[
            pl.BlockSpec((tq, MLA_HEADS * Q_HEAD_PAD), lambda b, i: (b * nq + i, 0)),
            pl.BlockSpec((seq, MLA_WIDTH), lambda b, i: (b, 0)),
            pl.BlockSpec((seq, LANE), lambda b, i: (b, 0)),
            pl.BlockSpec((MLA_WIDTH, seq), lambda b, i: (0, b)),
        ],
        out_specs=pl.BlockSpec((tq, MLA_WIDTH), lambda b, i: (b * nq + i, 0)),
        out_shape=jax.ShapeDtypeStruct((t, MLA_WIDTH), BF16),
        scratch_shapes=[
            pltpu.VMEM((MLA_HEADS, nq, tq, tq), F32),
            pltpu.VMEM((MLA_HEADS, 8, tq), F32),
            pltpu.VMEM((MLA_HEADS, V_HEAD + ONES_ROWS, tq), F32),
        ],
        compiler_params=pltpu.CompilerParams(
            dimension_semantics=("parallel", "arbitrary"), vmem_limit_bytes=VMEM_LIMIT),
        name="mla_attn",
    )(q, kn, kr, vt)


def _ssd_kernel(z_ref, xbc_ref, prev_ref, dt_ref, cw_ref, cb_ref, dtb_ref, alog_ref, dexp_ref, ng_ref,
                o_ref, st_ref, xc_ref, dte_ref, acse_ref, wb_ref, *, ts):
    s_idx = pl.program_id(1)
    halo = prev_ref.shape[0]

    @pl.when(s_idx == 0)
    def _():
        st_ref[...] = jnp.zeros_like(st_ref)

    sub = lax.broadcasted_iota(jnp.int32, (halo, SSM_CONV_CH), 0)
    for k in range(SSM_CONV):
        wb_ref[k] = jnp.broadcast_to(cw_ref[k:k + 1, :], (halo, SSM_CONV_CH))
    wb_ref[SSM_CONV] = jnp.broadcast_to(cb_ref[...], (halo, SSM_CONV_CH))

    def conv_tile(i, prev):
        r = pl.multiple_of(i * halo, halo)
        cur = xbc_ref[pl.ds(r, halo), :]
        conv = wb_ref[SSM_CONV] + wb_ref[SSM_CONV - 1] * cur
        for d in range(1, SSM_CONV):
            shifted = jnp.where(sub < d, pltpu.roll(prev, d, axis=0), pltpu.roll(cur, d, axis=0))
            conv = conv + wb_ref[SSM_CONV - 1 - d] * shifted
        xc_ref[pl.ds(r, halo), :] = conv * _sigmoid(conv)
        return cur

    lax.fori_loop(0, ts // halo, conv_tile, jnp.where(s_idx > 0, prev_ref[...], 0.0), unroll=True)

    raw = dt_ref[...] + dtb_ref[...]
    dt = jnp.maximum(raw, 0.0) + jnp.log1p(jnp.exp(-jnp.abs(raw)))
    a = dt * (-jnp.exp(alog_ref[...]))
    lane = lax.broadcasted_iota(jnp.int32, (ts, LANE), 1)
    ri = lax.broadcasted_iota(jnp.int32, (ts, ts), 0)
    ci = lax.broadcasted_iota(jnp.int32, (ts, ts), 1)
    tri = jnp.where((ri // CHUNK == ci // CHUNK) & (ci <= ri), 1.0, 0.0).astype(BF16)
    c3 = _dot(tri, _split3(a, lane))
    acs = c3 + pltpu.roll(c3, LANE - SSM_HEADS, axis=1) + pltpu.roll(c3, LANE - 2 * SSM_HEADS, axis=1)
    er = lax.broadcasted_iota(jnp.int32, (LANE, SSM_INNER), 0)
    ec = lax.broadcasted_iota(jnp.int32, (LANE, SSM_INNER), 1) // SSM_HEAD_DIM
    expand = jnp.where((er % SSM_HEADS == ec) & (er < 3 * SSM_HEADS), 1.0, 0.0).astype(BF16)
    log2e = float(np.log2(np.e))
    both = _dot(jnp.concatenate([_split3(dt, lane), _split3(acs * log2e, lane)], axis=0), expand)
    dte_ref[...] = both[:ts]
    acse_ref[...] = both[ts:]

    li = lax.broadcasted_iota(jnp.int32, (CHUNK, SSM_INNER), 0)
    si = lax.broadcasted_iota(jnp.int32, (CHUNK, SSM_INNER), 1) % SSM_HEAD_DIM
    quad = 4 * SSM_HEAD_DIM
    bd_r = lax.broadcasted_iota(jnp.int32, (quad, quad), 0) // SSM_HEAD_DIM
    bd_c = lax.broadcasted_iota(jnp.int32, (quad, quad), 1) // SSM_HEAD_DIM
    bd_mask = bd_r == bd_c
    dexp = dexp_ref[...]
    ng = ng_ref[...]

    def chunk_body(c, carry):
        r0 = pl.multiple_of(c * CHUNK, CHUNK)
        rows = pl.ds(r0, CHUNK)
        xs = xc_ref[rows, 0:SSM_INNER]
        acx = acse_ref[rows, :]
        last = acx[CHUNK - 1:CHUNK, :]
        xdt = xs * dte_ref[rows, :]
        rv = jnp.sum(jnp.where(li == si, acx, 0.0), axis=0, keepdims=True)
        lmat = jnp.exp2(jnp.where(li >= si, acx - rv, -jnp.inf))
        xdec = (xdt * jnp.exp2(last - acx)).astype(BF16)
        xdt_b = xdt.astype(BF16)
        eacx = jnp.exp2(acx)
        elast = jnp.exp2(last)
        ys = []
        for g in range(SSM_GROUPS):
            gl = slice(g * GROUP_WIDTH, (g + 1) * GROUP_WIDTH)
            b_g = xc_ref[rows, SSM_INNER + g * SSM_STATE:SSM_INNER + (g + 1) * SSM_STATE].astype(BF16)
            c0 = SSM_INNER + SSM_GROUPS * SSM_STATE + g * SSM_STATE
            c_g = xc_ref[rows, c0:c0 + SSM_STATE].astype(BF16)
            gt = _dot_nt(c_g, jnp.concatenate([b_g] * 4, axis=0))
            st_g = st_ref[:, gl]
            y_off = _dot(c_g, st_g.astype(BF16)) * eacx[:, gl]
            yd = []
            for qq in range(GROUP_WIDTH // quad):
                sl = slice(g * GROUP_WIDTH + qq * quad, g * GROUP_WIDTH + (qq + 1) * quad)
                m_q = (gt * lmat[:, sl]).astype(BF16)
                x_q = jnp.concatenate([xdt_b[:, sl]] * 4, axis=0)
                x_q = jnp.where(bd_mask, x_q, jnp.zeros_like(x_q))
                yd.append(_dot(m_q, x_q))
            ys.append(jnp.concatenate(yd, axis=-1) + y_off)
            st_ref[:, gl] = st_g * elast[:, gl] + _dot_tn(b_g, xdec[:, gl])
        y = jnp.concatenate(ys, axis=-1) + xs * dexp
        z = z_ref[rows, :]
        y = y * (z * _sigmoid(z))
        outs = []
        for g in range(SSM_GROUPS):
            yg = y[:, g * GROUP_WIDTH:(g + 1) * GROUP_WIDTH]
            outs.append(yg * lax.rsqrt(jnp.mean(yg * yg, axis=-1, keepdims=True) + EPS))
        o_ref[rows, :] = (jnp.concatenate(outs, axis=-1) * ng).astype(BF16)
        return carry

    lax.fori_loop(0, ts // CHUNK, chunk_body, 0, unroll=True)


def _ssd(u, cw, cb, dtb, alog, dexp, ng, batch, seq, ts=256, halo=8):
    t = u.shape[0]
    ns = seq // ts
    const = lambda b, s: (0, 0)
    rowblk = lambda b, s: b * ns + s
    return pl.pallas_call(
        functools.partial(_ssd_kernel, ts=ts),
        grid=(batch, ns),
        in_specs=[
            pl.BlockSpec((ts, SSM_INNER), lambda b, s: (rowblk(b, s), U_Z // SSM_INNER)),
            pl.BlockSpec((ts, SSM_CONV_CH), lambda b, s: (rowblk(b, s), U_XBC // SSM_CONV_CH)),
            pl.BlockSpec((halo, SSM_CONV_CH),
                         lambda b, s: (jnp.maximum(rowblk(b, s) * (ts // halo) - 1, 0), U_XBC // SSM_CONV_CH)),
            pl.BlockSpec((ts, LANE), lambda b, s: (rowblk(b, s), U_DT // LANE)),
            pl.BlockSpec((SSM_CONV, SSM_CONV_CH), const),
            pl.BlockSpec((1, SSM_CONV_CH), const),
            pl.BlockSpec((1, LANE), const),
            pl.BlockSpec((1, LANE), const),
            pl.BlockSpec((1, SSM_INNER), const),
            pl.BlockSpec((1, SSM_INNER), const),
        ],
        out_specs=pl.BlockSpec((ts, SSM_INNER), lambda b, s: (rowblk(b, s), 0)),
        out_shape=jax.ShapeDtypeStruct((t, SSM_INNER), BF16),
        scratch_shapes=[
            pltpu.VMEM((SSM_STATE, SSM_INNER), F32),
            pltpu.VMEM((ts, SSM_CONV_CH), F32),
            pltpu.VMEM((ts, SSM_INNER), F32),
            pltpu.VMEM((ts, SSM_INNER), F32),
            pltpu.VMEM((SSM_CONV + 1, halo, SSM_CONV_CH), F32),
        ],
        compiler_params=pltpu.CompilerParams(
            dimension_semantics=("parallel", "arbitrary"), vmem_limit_bytes=VMEM_LIMIT),
        name="ssd",
    )(u, u, u, u, cw, cb, dtb, alog, dexp, ng)


def _out_proj_kernel(a_ref, b_ref, x_ref, wa_ref, wb_ref, g1_ref, g2_ref, x1_ref, h2_ref, *, piece):
    for r in range(0, a_ref.shape[0], piece):
        rows = slice(r, r + piece)
        mix = _dot(a_ref[rows, :], wa_ref[...]) + _dot(b_ref[rows, :], wb_ref[...])
        x1 = x_ref[rows, :] + _rms(mix, g1_ref[...])
        x1_ref[rows, :] = x1
        h2_ref[rows, :] = _rms(x1, g2_ref[...]).astype(BF16)


def _out_proj(a_out, b_out, x2, wa, wb, g1, g2, tm=512, piece=256):
    t = x2.shape[0]
    row = lambda i: (i, 0)
    const = lambda i: (0, 0)
    return pl.pallas_call(
        functools.partial(_out_proj_kernel, piece=piece),
        grid=(t // tm,),
        in_specs=[
            pl.BlockSpec((tm, MLA_WIDTH), row),
            pl.BlockSpec((tm, SSM_INNER), row),
            pl.BlockSpec((tm, D_MODEL), row),
            pl.BlockSpec((MLA_WIDTH, D_MODEL), const),
            pl.BlockSpec((SSM_INNER, D_MODEL), const),
            pl.BlockSpec((1, D_MODEL), const),
            pl.BlockSpec((1, D_MODEL), const),
        ],
        out_specs=[pl.BlockSpec((tm, D_MODEL), row), pl.BlockSpec((tm, D_MODEL), row)],
        out_shape=[jax.ShapeDtypeStruct((t, D_MODEL), F32), jax.ShapeDtypeStruct((t, D_MODEL), BF16)],
        compiler_params=pltpu.CompilerParams(
            dimension_semantics=("parallel",), vmem_limit_bytes=VMEM_LIMIT),
        name="out_proj",
    )(a_out, b_out, x2, wa, wb, g1, g2)


def _ffn_kernel(h_ref, x1_ref, wg_ref, wu_ref, wd_ref, cw_ref, g_ref, o_ref,
                acc_ref, act_ref, tail_ref, *, tm, tf, seq):
    i = pl.program_id(0)
    j = pl.program_id(1)
    nt = pl.num_programs(0) - 1
    nf = tail_ref.shape[0]
    sub = tail_ref.shape[1]
    first = (i == 0) & (j == 0)
    cols = pl.ds(pl.multiple_of(j * tf, tf), tf)

    def produce(slot):
        cw = cw_ref[:, cols]
        prev = tail_ref[j]
        row = lax.broadcasted_iota(jnp.int32, prev.shape, 0)
        c0 = float(np.sqrt(2.0 / np.pi))
        for r in range(0, tm, FFN_ROWS):
            rows = slice(r, r + FFN_ROWS)
            gate = _dot(h_ref[rows, :], wg_ref[...])
            up = _dot(h_ref[rows, :], wu_ref[...])
            pos = (i * tm + r) % seq + lax.broadcasted_iota(jnp.int32, (FFN_ROWS, 1), 0)
            conv = cw[FFN_CONV:FFN_CONV + 1, :] + cw[FFN_CONV - 1:FFN_CONV, :] * gate
            for k in range(FFN_CONV - 1):
                d = FFN_CONV - 1 - k
                rolled = pltpu.roll(gate, d, axis=0)
                head = jnp.where(row < d, pltpu.roll(prev, d, axis=0), rolled[:sub, :])
                shifted = jnp.concatenate([head, rolled[sub:, :]], axis=0)
                conv = conv + cw[k:k + 1, :] * jnp.where(pos >= d, shifted, 0.0)
            act = 0.5 * conv * (1.0 + jnp.tanh(c0 * (conv + 0.044715 * (conv * conv * conv))))
            act_ref[slot, rows, :] = (act * up).astype(BF16)
            prev = gate[FFN_ROWS - sub:, :]
        tail_ref[j] = prev

    def consume(slot):
        acc_ref[...] += _dot(act_ref[slot], wd_ref[...])

    @pl.when(first)
    def _():
        tail_ref[...] = jnp.zeros_like(tail_ref)
        produce(0)

    @pl.when(j == 1)
    def _():
        acc_ref[...] = jnp.zeros_like(acc_ref)

    for parity in range(2):
        @pl.when(jnp.logical_not(first) & (i < nt) & ((i + j) % 2 == parity))
        def _(parity=parity):
            produce(parity)
            consume(1 - parity)

    @pl.when((i == nt) & (j == 0))
    def _():
        consume((nt * nf - 1) % 2)

    @pl.when(jnp.logical_not(first) & (j == 0))
    def _():
        o_ref[...] = x1_ref[...] + _rms(acc_ref[...], g_ref[...])


def _ffn(h2, x1, wg, wu, wd, cwb, g, seq, tm=512, tf=512):
    t = h2.shape[0]
    nt = t // tm
    nf = D_FF // tf
    assert nf % 2 == 1
    row_p = lambda i, j: (jnp.minimum(i, nt - 1), 0)
    col_p = lambda i, j: (jnp.where(i < nt, j, nf - 1), 0, 0)
    wg, wu = [w.reshape(D_MODEL, nf, tf).transpose(1, 0, 2) for w in (wg, wu)]
    row_c = lambda i, j: (jnp.where(j > 0, jnp.minimum(i, nt - 1), jnp.maximum(i - 1, 0)), 0)
    return pl.pallas_call(
        functools.partial(_ffn_kernel, tm=tm, tf=tf, seq=seq),
        grid=(nt + 1, nf),
        in_specs=[
            pl.BlockSpec((tm, D_MODEL), row_p),
            pl.BlockSpec((tm, D_MODEL), row_c),
            pl.BlockSpec((None, D_MODEL, tf), col_p),
            pl.BlockSpec((None, D_MODEL, tf), col_p),
            pl.BlockSpec((tf, D_MODEL), lambda i, j: (jnp.where(j > 0, j - 1, nf - 1), 0)),
            pl.BlockSpec((FFN_CONV + 1, D_FF), lambda i, j: (0, 0)),
            pl.BlockSpec((1, D_MODEL), lambda i, j: (0, 0)),
        ],
        out_specs=pl.BlockSpec((tm, D_MODEL), row_c),
        out_shape=jax.ShapeDtypeStruct((t, D_MODEL), F32),
        scratch_shapes=[
            pltpu.VMEM((tm, D_MODEL), F32),
            pltpu.VMEM((2, tm, tf), BF16),
            pltpu.VMEM((nf, 8, tf), F32),
        ],
        compiler_params=pltpu.CompilerParams(
            dimension_semantics=("arbitrary", "arbitrary"), vmem_limit_bytes=VMEM_LIMIT),
        name="ffn",
    )(h2, x1, wg, wu, wd, cwb, g)


def _swap_half(w):
    half = w.shape[-1] // 2
    return jnp.concatenate([-w[..., half:], w[..., :half]], axis=-1)


def _rope_tables(seq):
    inv = 1.0 / (ROPE_THETA ** (jnp.arange(0, QK_ROPE, 2, dtype=F32) / QK_ROPE))
    ang = jnp.arange(seq, dtype=F32)[:, None] * inv[None, :]
    zeros = jnp.zeros((seq, LANE - QK_ROPE), F32)
    cosp = jnp.concatenate([jnp.cos(ang), jnp.cos(ang), zeros], axis=-1)
    sinp = jnp.concatenate([jnp.sin(ang), jnp.sin(ang), zeros], axis=-1)
    return cosp, sinp


def _layer(x2, batch, seq, mix_pre_g, w_in, q_norm_g, w_uq, kv_norm_g, w_ukv, ssm_conv_w, ssm_conv_b,
           dt_bias, a_log, d_skip, ssm_norm_g, w_out, mix_post_g, ffn_pre_g, w_gate, w_up,
           ffn_conv_w, ffn_conv_b, w_down, ffn_post_g):
    row = lambda v: v.reshape(1, -1).astype(F32)
    pad_lane = lambda v: jnp.pad(v.astype(F32), (0, LANE - v.shape[0])).reshape(1, LANE)

    cuts = [0] + np.cumsum([Q_LORA, KV_LORA, QK_ROPE, SSM_INNER, SSM_CONV_CH, SSM_HEADS]).tolist()
    w_cq, w_ckv, w_kr, w_z, w_xbc, w_dt = [w_in[:, a:b] for a, b in zip(cuts[:-1], cuts[1:])]
    w_in_r = jnp.concatenate(
        [w_cq, w_kr, _swap_half(w_kr), w_dt, jnp.zeros((D_MODEL, LANE - SSM_HEADS), w_in.dtype),
         w_ckv, w_xbc, w_z], axis=-1).astype(BF16)

    wq3 = w_uq.reshape(Q_LORA, MLA_HEADS, QK_NOPE + QK_ROPE)
    wq_rope = wq3[..., QK_NOPE:]
    wq_r = jnp.concatenate([wq3[..., :QK_NOPE], wq_rope, _swap_half(wq_rope)], axis=-1)
    wq_r = wq_r.reshape(Q_LORA, MLA_HEADS * Q_HEAD_PAD).astype(BF16)
    wkv3 = w_ukv.reshape(KV_LORA, MLA_HEADS, QK_NOPE + V_HEAD)
    wk_r = wkv3[..., :QK_NOPE].reshape(KV_LORA, MLA_WIDTH).astype(BF16)
    wvt_r = wkv3[..., QK_NOPE:].reshape(KV_LORA, MLA_WIDTH).T.astype(BF16)

    u = _in_proj(x2, row(mix_pre_g), w_in_r)
    cosp, sinp = _rope_tables(seq)
    q, kn, kr, vt = _mla_proj(u, cosp, sinp, row(q_norm_g), row(kv_norm_g), wq_r, wk_r, wvt_r, seq)
    a_out = _mla_attn(q, kn, kr, vt, batch, seq)
    b_out = _ssd(u, ssm_conv_w.astype(F32), row(ssm_conv_b), pad_lane(dt_bias), pad_lane(a_log),
                 row(jnp.repeat(d_skip, SSM_HEAD_DIM)), row(ssm_norm_g), batch, seq)
    w_out_b = w_out.astype(BF16)
    x1, h2 = _out_proj(a_out, b_out, x2, w_out_b[:MLA_WIDTH], w_out_b[MLA_WIDTH:],
                       row(mix_post_g), row(ffn_pre_g))
    return _ffn(h2, x1, w_gate.astype(BF16), w_up.astype(BF16), w_down.astype(BF16),
                jnp.concatenate([ffn_conv_w.astype(F32), row(ffn_conv_b)], axis=0), row(ffn_post_g), seq)


def kernel(x, mix_pre_g, w_in, q_norm_g, w_uq, kv_norm_g, w_ukv, ssm_conv_w, ssm_conv_b, dt_bias, a_log,
           d_skip, ssm_norm_g, w_out, mix_post_g, ffn_pre_g, w_gate, w_up, ffn_conv_w, ffn_conv_b,
           w_down, ffn_post_g):
    batch, seq, _ = x.shape
    x2 = x.reshape(batch * seq, D_MODEL)
    for l in range(mix_pre_g.shape[0]):
        x2 = _layer(x2, batch, seq, mix_pre_g[l], w_in[l], q_norm_g[l], w_uq[l], kv_norm_g[l], w_ukv[l],
                    ssm_conv_w[l], ssm_conv_b[l], dt_bias[l], a_log[l], d_skip[l], ssm_norm_g[l],
                    w_out[l], mix_post_g[l], ffn_pre_g[l], w_gate[l], w_up[l], ffn_conv_w[l],
                    ffn_conv_b[l], w_down[l], ffn_post_g[l])
    return x2.reshape(batch, seq, D_MODEL)
```

```python
import functools

import jax
import jax.numpy as jnp
import numpy as np
from jax import lax
from jax.experimental import pallas as pl
from jax.experimental.pallas import tpu as pltpu

F32 = jnp.float32
BF16 = jnp.bfloat16

D_MODEL = 2048
CHUNK = 64
EPS = 1e-6

MLA_HEADS = 8
Q_LORA = 768
KV_LORA = 512
QK_NOPE = 128
QK_ROPE = 64
V_HEAD = 128
ROPE_THETA = 10000.0
MLA_WIDTH = MLA_HEADS * V_HEAD
Q_HEAD_PAD = 256

SSM_HEADS = 16
SSM_HEAD_DIM = 64
SSM_INNER = SSM_HEADS * SSM_HEAD_DIM
SSM_GROUPS = 2
SSM_STATE = 128
SSM_CONV = 4
SSM_CONV_CH = SSM_INNER + 2 * SSM_GROUPS * SSM_STATE
GROUP_WIDTH = SSM_INNER // SSM_GROUPS

D_FF = 5632
FFN_CONV = 3
FFN_ROWS = 256

LANE = 128
U_COLS = 4096
U_CQ = 0
U_KR = Q_LORA
U_DT = U_KR + LANE
U_CKV = U_DT + LANE
U_XBC = U_CKV + KV_LORA
U_Z = U_XBC + SSM_CONV_CH

VMEM_LIMIT = 56 * 1024 * 1024


def _rms(x, g):
    return x * lax.rsqrt(jnp.mean(x * x, axis=-1, keepdims=True) + EPS) * g


def _sigmoid(x):
    return 1.0 / (1.0 + jnp.exp(-x))


def _dot(a, b):
    return jnp.dot(a, b, preferred_element_type=F32)


def _dot_nt(a, b):
    return lax.dot_general(a, b, (((1,), (1,)), ((), ())), preferred_element_type=F32)


def _dot_tn(a, b):
    return lax.dot_general(a, b, (((0,), (0,)), ((), ())), preferred_element_type=F32)


def _split3(x, lane):
    xm = jnp.where(lane < SSM_HEADS, x, 0.0)
    hi = xm.astype(BF16).astype(F32)
    r1 = xm - hi
    mid = r1.astype(BF16).astype(F32)
    lo = (r1 - mid).astype(BF16).astype(F32)
    packed = hi + pltpu.roll(mid, SSM_HEADS, axis=1) + pltpu.roll(lo, 2 * SSM_HEADS, axis=1)
    return packed.astype(BF16)


def _in_proj_kernel(x_ref, g_ref, w_ref, o_ref):
    x = x_ref[...]
    r = lax.rsqrt(jnp.mean(x * x, axis=-1, keepdims=True) + EPS)
    o_ref[...] = _dot((x * g_ref[...]).astype(BF16), w_ref[...]) * r


def _in_proj(x2, g, w, tm=512):
    t = x2.shape[0]
    return pl.pallas_call(
        _in_proj_kernel,
        grid=(t // tm,),
        in_specs=[
            pl.BlockSpec((tm, D_MODEL), lambda i: (i, 0)),
            pl.BlockSpec((1, D_MODEL), lambda i: (0, 0)),
            pl.BlockSpec((D_MODEL, U_COLS), lambda i: (0, 0), pipeline_mode=pl.Buffered(1)),
        ],
        out_specs=pl.BlockSpec((tm, U_COLS), lambda i: (i, 0)),
        out_shape=jax.ShapeDtypeStruct((t, U_COLS), F32),
        compiler_params=pltpu.CompilerParams(
            dimension_semantics=("parallel",), vmem_limit_bytes=VMEM_LIMIT),
        name="in_proj",
    )(x2, g, w)


def _rope_pair(blk, cosp, sinp):
    return blk * cosp + pltpu.roll(blk, QK_ROPE, axis=1) * sinp


def _mla_proj_kernel(cq_ref, ckv_ref, kr_ref, cos_ref, sin_ref, qg_ref, kvg_ref, wq_ref, wk_ref, wvt_ref,
                     q_ref, kn_ref, krot_ref, vt_ref, *, scale):
    cosp = cos_ref[...]
    sinp = sin_ref[...]
    qall = _dot(_rms(cq_ref[...], qg_ref[...]).astype(BF16), wq_ref[...])
    for h in range(MLA_HEADS):
        base = h * Q_HEAD_PAD
        q_ref[:, base:base + QK_NOPE] = (qall[:, base:base + QK_NOPE] * scale).astype(BF16)
        rot = _rope_pair(qall[:, base + QK_NOPE:base + Q_HEAD_PAD], cosp, sinp)
        q_ref[:, base + QK_NOPE:base + Q_HEAD_PAD] = (rot * scale).astype(BF16)
    ckv = _rms(ckv_ref[...], kvg_ref[...]).astype(BF16)
    kn_ref[...] = _dot(ckv, wk_ref[...]).astype(BF16)
    vt_ref[...] = _dot_nt(wvt_ref[...], ckv).astype(BF16)
    krot_ref[...] = _rope_pair(kr_ref[...], cosp, sinp).astype(BF16)


def _mla_proj(u, cosp, sinp, qg, kvg, wq, wk, wvt, seq, tm=512):
    t = u.shape[0]
    nseq = seq // tm
    scale = float((QK_NOPE + QK_ROPE) ** -0.5 * np.log2(np.e))
    row = lambda i: (i, 0)
    const = lambda i: (0, 0)
    return pl.pallas_call(
        functools.partial(_mla_proj_kernel, scale=scale),
        grid=(t // tm,),
        in_specs=[
            pl.BlockSpec((tm, Q_LORA), lambda i: (i, U_CQ // Q_LORA)),
            pl.BlockSpec((tm, KV_LORA), lambda i: (i, U_CKV // KV_LORA)),
            pl.BlockSpec((tm, LANE), lambda i: (i, U_KR // LANE)),
            pl.BlockSpec((tm, LANE), lambda i: (i % nseq, 0)),
            pl.BlockSpec((tm, LANE), lambda i: (i % nseq, 0)),
            pl.BlockSpec((1, Q_LORA), const),
            pl.BlockSpec((1, KV_LORA), const),
            pl.BlockSpec((Q_LORA, MLA_HEADS * Q_HEAD_PAD), const),
            pl.BlockSpec((KV_LORA, MLA_WIDTH), const),
            pl.BlockSpec((MLA_WIDTH, KV_LORA), const),
        ],
        out_specs=[
            pl.BlockSpec((tm, MLA_HEADS * Q_HEAD_PAD), row),
            pl.BlockSpec((tm, MLA_WIDTH), row),
            pl.BlockSpec((tm, LANE), row),
            pl.BlockSpec((MLA_WIDTH, tm), lambda i: (0, i)),
        ],
        out_shape=[
            jax.ShapeDtypeStruct((t, MLA_HEADS * Q_HEAD_PAD), BF16),
            jax.ShapeDtypeStruct((t, MLA_WIDTH), BF16),
            jax.ShapeDtypeStruct((t, LANE), BF16),
            jax.ShapeDtypeStruct((MLA_WIDTH, t), BF16),
        ],
        compiler_params=pltpu.CompilerParams(
            dimension_semantics=("parallel",), vmem_limit_bytes=VMEM_LIMIT),
        name="mla_proj",
    )(u, u, u, cosp, sinp, qg, kvg, wq, wk, wvt)


ONES_ROWS = 16


def _attn_kernel(q_ref, kn_ref, kr_ref, vt_ref, o_ref, s_ref, mrun_ref, acc_ref, *, tq):
    qi = pl.program_id(1)
    sublanes = mrun_ref.shape[1]
    krow = lax.broadcasted_iota(jnp.int32, (tq, tq), 0) // CHUNK
    qcol = lax.broadcasted_iota(jnp.int32, (tq, tq), 1) // CHUNK
    diag_mask = krow <= qcol
    ones = jnp.ones((ONES_ROWS, tq), BF16)

    mrun_ref[...] = jnp.full(mrun_ref.shape, -1e30, F32)
    acc_ref[...] = jnp.zeros_like(acc_ref)

    def scores(j, masked):
        ks = pl.multiple_of(j * tq, tq)
        kr = kr_ref[pl.ds(ks, tq), :]
        for h in range(MLA_HEADS):
            q = q_ref[:, h * Q_HEAD_PAD:(h + 1) * Q_HEAD_PAD]
            k = jnp.concatenate([kn_ref[pl.ds(ks, tq), h * QK_NOPE:(h + 1) * QK_NOPE], kr], axis=-1)
            s = _dot_nt(k, q)
            if masked:
                s = jnp.where(diag_mask, s, -1e30)
            s_ref[h, j] = s
            tile_max = jnp.max(s.reshape(tq // sublanes, sublanes, tq), axis=0)
            mrun_ref[h] = jnp.maximum(mrun_ref[h], tile_max)

    def scores_body(j, carry):
        scores(j, masked=False)
        return carry

    lax.fori_loop(0, qi, scores_body, 0)
    scores(qi, masked=True)

    m = [jnp.max(mrun_ref[h], axis=0, keepdims=True) for h in range(MLA_HEADS)]

    def accumulate(j, carry):
        ks = pl.multiple_of(j * tq, tq)
        for h in range(MLA_HEADS):
            p = jnp.exp2(s_ref[h, j] - m[h]).astype(BF16)
            vt = jnp.concatenate([vt_ref[h * V_HEAD:(h + 1) * V_HEAD, pl.ds(ks, tq)], ones], axis=0)
            acc_ref[h] += _dot(vt, p)
        return carry

    lax.fori_loop(0, qi + 1, accumulate, 0)
    for h in range(MLA_HEADS):
        acc = acc_ref[h]
        out_t = acc[:V_HEAD, :] / acc[V_HEAD:V_HEAD + 1, :]
        o_ref[:, h * V_HEAD:(h + 1) * V_HEAD] = out_t.T.astype(BF16)


def _mla_attn(q, kn, kr, vt, batch, seq, tq=256):
    t = q.shape[0]
    nq = seq // tq
    return pl.pallas_call(
        functools.partial(_attn_kernel, tq=tq),
        grid=(batch, nq),
        in_specs=[
            pl.BlockSpec((tq, MLA_HEADS * Q_HEAD_PAD), lambda b, i: (b * nq + i, 0)),
            pl.BlockSpec((seq, MLA_WIDTH), lambda b, i: (b, 0)),
            pl.BlockSpec((seq, LANE), lambda b, i: (b, 0)),
            pl.BlockSpec((MLA_WIDTH, seq), lambda b, i: (0, b)),
        ],
        out_specs=pl.BlockSpec((tq, MLA_WIDTH), lambda b, i: (b * nq + i, 0)),
        out_shape=jax.ShapeDtypeStruct((t, MLA_WIDTH), BF16),
        scratch_shapes=[
            pltpu.VMEM((MLA_HEADS, nq, tq, tq), F32),
            pltpu.VMEM((MLA_HEADS, 8, tq), F32),
            pltpu.VMEM((MLA_HEADS, V_HEAD + ONES_ROWS, tq), F32),
        ],
        compiler_params=pltpu.CompilerParams(
            dimension_semantics=("parallel", "arbitrary"), vmem_limit_bytes=VMEM_LIMIT),
        name="mla_attn",
    )(q, kn, kr, vt)


def _ssd_kernel(z_ref, xbc_ref, prev_ref, dt_ref, cw_ref, cb_ref, dtb_ref, alog_ref, dexp_ref, ng_ref,
                o_ref, st_ref, xc_ref, dte_ref, acse_ref, wb_ref, *, ts):
    s_idx = pl.program_id(1)
    halo = prev_ref.shape[0]

    @pl.when(s_idx == 0)
    def _():
        st_ref[...] = jnp.zeros_like(st_ref)

    sub = lax.broadcasted_iota(jnp.int32, (halo, SSM_CONV_CH), 0)
    for k in range(SSM_CONV):
        wb_ref[k] = jnp.broadcast_to(cw_ref[k:k + 1, :], (halo, SSM_CONV_CH))
    wb_ref[SSM_CONV] = jnp.broadcast_to(cb_ref[...], (halo, SSM_CONV_CH))

    def conv_tile(i, prev):
        r = pl.multiple_of(i * halo, halo)
        cur = xbc_ref[pl.ds(r, halo), :]
        conv = wb_ref[SSM_CONV] + wb_ref[SSM_CONV - 1] * cur
        for d in range(1, SSM_CONV):
            shifted = jnp.where(sub < d, pltpu.roll(prev, d, axis=0), pltpu.roll(cur, d, axis=0))
            conv = conv + wb_ref[SSM_CONV - 1 - d] * shifted
        xc_ref[pl.ds(r, halo), :] = conv * _sigmoid(conv)
        return cur

    lax.fori_loop(0, ts // halo, conv_tile, jnp.where(s_idx > 0, prev_ref[...], 0.0), unroll=True)

    raw = dt_ref[...] + dtb_ref[...]
    dt = jnp.maximum(raw, 0.0) + jnp.log1p(jnp.exp(-jnp.abs(raw)))
    a = dt * (-jnp.exp(alog_ref[...]))
    lane = lax.broadcasted_iota(jnp.int32, (ts, LANE), 1)
    ri = lax.broadcasted_iota(jnp.int32, (ts, ts), 0)
    ci = lax.broadcasted_iota(jnp.int32, (ts, ts), 1)
    tri = jnp.where((ri // CHUNK == ci // CHUNK) & (ci <= ri), 1.0, 0.0).astype(BF16)
    c3 = _dot(tri, _split3(a, lane))
    acs = c3 + pltpu.roll(c3, LANE - SSM_HEADS, axis=1) + pltpu.roll(c3, LANE - 2 * SSM_HEADS, axis=1)
    er = lax.broadcasted_iota(jnp.int32, (LANE, SSM_INNER), 0)
    ec = lax.broadcasted_iota(jnp.int32, (LANE, SSM_INNER), 1) // SSM_HEAD_DIM
    expand = jnp.where((er % SSM_HEADS == ec) & (er < 3 * SSM_HEADS), 1.0, 0.0).astype(BF16)
    log2e = float(np.log2(np.e))
    both = _dot(jnp.concatenate([_split3(dt, lane), _split3(acs * log2e, lane)], axis=0), expand)
    dte_ref[...] = both[:ts]
    acse_ref[...] = both[ts:]

    li = lax.broadcasted_iota(jnp.int32, (CHUNK, SSM_INNER), 0)
    si = lax.broadcasted_iota(jnp.int32, (CHUNK, SSM_INNER), 1) % SSM_HEAD_DIM
    quad = 4 * SSM_HEAD_DIM
    bd_r = lax.broadcasted_iota(jnp.int32, (quad, quad), 0) // SSM_HEAD_DIM
    bd_c = lax.broadcasted_iota(jnp.int32, (quad, quad), 1) // SSM_HEAD_DIM
    bd_mask = bd_r == bd_c
    dexp = dexp_ref[...]
    ng = ng_ref[...]

    def chunk_body(c, carry):
        r0 = pl.multiple_of(c * CHUNK, CHUNK)
        rows = pl.ds(r0, CHUNK)
        xs = xc_ref[rows, 0:SSM_INNER]
        acx = acse_ref[rows, :]
        last = acx[CHUNK - 1:CHUNK, :]
        xdt = xs * dte_ref[rows, :]
        rv = jnp.sum(jnp.where(li == si, acx, 0.0), axis=0, keepdims=True)
        lmat = jnp.exp2(jnp.where(li >= si, acx - rv, -jnp.inf))
        xdec = (xdt * jnp.exp2(last - acx)).astype(BF16)
        xdt_b = xdt.astype(BF16)
        eacx = jnp.exp2(acx)
        elast = jnp.exp2(last)
        ys = []
        for g in range(SSM_GROUPS):
            gl = slice(g * GROUP_WIDTH, (g + 1) * GROUP_WIDTH)
            b_g = xc_ref[rows, SSM_INNER + g * SSM_STATE:SSM_INNER + (g + 1) * SSM_STATE].astype(BF16)
            c0 = SSM_INNER + SSM_GROUPS * SSM_STATE + g * SSM_STATE
            c_g = xc_ref[rows, c0:c0 + SSM_STATE].astype(BF16)
            gt = _dot_nt(c_g, jnp.concatenate([b_g] * 4, axis=0))
            st_g = st_ref[:, gl]
            y_off = _dot(c_g, st_g.astype(BF16)) * eacx[:, gl]
            yd = []
            for qq in range(GROUP_WIDTH // quad):
                sl = slice(g * GROUP_WIDTH + qq * quad, g * GROUP_WIDTH + (qq + 1) * quad)
                m_q = (gt * lmat[:, sl]).astype(BF16)
                x_q = jnp.concatenate([xdt_b[:, sl]] * 4, axis=0)
                x_q = jnp.where(bd_mask, x_q, jnp.zeros_like(x_q))
                yd.append(_dot(m_q, x_q))
            ys.append(jnp.concatenate(yd, axis=-1) + y_off)
            st_ref[:, gl] = st_g * elast[:, gl] + _dot_tn(b_g, xdec[:, gl])
        y = jnp.concatenate(ys, axis=-1) + xs * dexp
        z = z_ref[rows, :]
        y = y * (z * _sigmoid(z))
        outs = []
        for g in range(SSM_GROUPS):
            yg = y[:, g * GROUP_WIDTH:(g + 1) * GROUP_WIDTH]
            outs.append(yg * lax.rsqrt(jnp.mean(yg * yg, axis=-1, keepdims=True) + EPS))
        o_ref[rows, :] = (jnp.concatenate(outs, axis=-1) * ng).astype(BF16)
        return carry

    lax.fori_loop(0, ts // CHUNK, chunk_body, 0, unroll=True)


def _ssd(u, cw, cb, dtb, alog, dexp, ng, batch, seq, ts=256, halo=8):
    t = u.shape[0]
    ns = seq // ts
    const = lambda b, s: (0, 0)
    rowblk = lambda b, s: b * ns + s
    return pl.pallas_call(
        functools.partial(_ssd_kernel, ts=ts),
        grid=(batch, ns),
        in_specs=[
            pl.BlockSpec((ts, SSM_INNER), lambda b, s: (rowblk(b, s), U_Z // SSM_INNER)),
            pl.BlockSpec((ts, SSM_CONV_CH), lambda b, s: (rowblk(b, s), U_XBC // SSM_CONV_CH)),
            pl.BlockSpec((halo, SSM_CONV_CH),
                         lambda b, s: (jnp.maximum(rowblk(b, s) * (ts // halo) - 1, 0), U_XBC // SSM_CONV_CH)),
            pl.BlockSpec((ts, LANE), lambda b, s: (rowblk(b, s), U_DT // LANE)),
            pl.BlockSpec((SSM_CONV, SSM_CONV_CH), const),
            pl.BlockSpec((1, SSM_CONV_CH), const),
            pl.BlockSpec((1, LANE), const),
            pl.BlockSpec((1, LANE), const),
            pl.BlockSpec((1, SSM_INNER), const),
            pl.BlockSpec((1, SSM_INNER), const),
        ],
        out_specs=pl.BlockSpec((ts, SSM_INNER), lambda b, s: (rowblk(b, s), 0)),
        out_shape=jax.ShapeDtypeStruct((t, SSM_INNER), BF16),
        scratch_shapes=[
            pltpu.VMEM((SSM_STATE, SSM_INNER), F32),
            pltpu.VMEM((ts, SSM_CONV_CH), F32),
            pltpu.VMEM((ts, SSM_INNER), F32),
            pltpu.VMEM((ts, SSM_INNER), F32),
            pltpu.VMEM((SSM_CONV + 1, halo, SSM_CONV_CH), F32),
        ],
        compiler_params=pltpu.CompilerParams(
            dimension_semantics=("parallel", "arbitrary"), vmem_limit_bytes=VMEM_LIMIT),
        name="ssd",
    )(u, u, u, u, cw, cb, dtb, alog, dexp, ng)


def _out_proj_kernel(a_ref, b_ref, x_ref, wa_ref, wb_ref, g1_ref, g2_ref, x1_ref, h2_ref, *, piece):
    for r in range(0, a_ref.shape[0], piece):
        rows = slice(r, r + piece)
        mix = _dot(a_ref[rows, :], wa_ref[...]) + _dot(b_ref[rows, :], wb_ref[...])
        x1 = x_ref[rows, :] + _rms(mix, g1_ref[...])
        x1_ref[rows, :] = x1
        h2_ref[rows, :] = _rms(x1, g2_ref[...]).astype(BF16)


def _out_proj(a_out, b_out, x2, wa, wb, g1, g2, tm=512, piece=256):
    t = x2.shape[0]
    row = lambda i: (i, 0)
    const = lambda i: (0, 0)
    return pl.pallas_call(
        functools.partial(_out_proj_kernel, piece=piece),
        grid=(t // tm,),
        in_specs=[
            pl.BlockSpec((tm, MLA_WIDTH), row),
            pl.BlockSpec((tm, SSM_INNER), row),
            pl.BlockSpec((tm, D_MODEL), row),
            pl.BlockSpec((MLA_WIDTH, D_MODEL), const),
            pl.BlockSpec((SSM_INNER, D_MODEL), const),
            pl.BlockSpec((1, D_MODEL), const),
            pl.BlockSpec((1, D_MODEL), const),
        ],
        out_specs=[pl.BlockSpec((tm, D_MODEL), row), pl.BlockSpec((tm, D_MODEL), row)],
        out_shape=[jax.ShapeDtypeStruct((t, D_MODEL), F32), jax.ShapeDtypeStruct((t, D_MODEL), BF16)],
        compiler_params=pltpu.CompilerParams(
            dimension_semantics=("parallel",), vmem_limit_bytes=VMEM_LIMIT),
        name="out_proj",
    )(a_out, b_out, x2, wa, wb, g1, g2)


def _ffn_kernel(h_ref, x1_hbm, wg_ref, wu_ref, wd_ref, cw_ref, g_ref, o_ref,
                act_ref, tail_ref, x1_ref, x1_sem, *, tm, tf, seq):
    i = pl.program_id(0)
    j = pl.program_id(1)
    nt = pl.num_programs(0) - 1
    nf = tail_ref.shape[0]
    sub = tail_ref.shape[1]
    first = (i == 0) & (j == 0)
    cols = pl.ds(pl.multiple_of(j * tf, tf), tf)

    def x1_copy(tile):
        rows = pl.ds(pl.multiple_of(tile * tm, tm), tm)
        return pltpu.make_async_copy(x1_hbm.at[rows, :], x1_ref, x1_sem.at[0])

    @pl.when((i < nt) & (j == nf - 1))
    def _():
        x1_copy(i).start()

    def produce(slot):
        cw = cw_ref[:, cols]
        prev = tail_ref[j]
        row = lax.broadcasted_iota(jnp.int32, prev.shape, 0)
        c0 = float(np.sqrt(2.0 / np.pi))
        for r in range(0, tm, FFN_ROWS):
            rows = slice(r, r + FFN_ROWS)
            gate = _dot(h_ref[rows, :], wg_ref[...])
            up = _dot(h_ref[rows, :], wu_ref[...])
            pos = (i * tm + r) % seq + lax.broadcasted_iota(jnp.int32, (FFN_ROWS, 1), 0)
            conv = cw[FFN_CONV:FFN_CONV + 1, :] + cw[FFN_CONV - 1:FFN_CONV, :] * gate
            for k in range(FFN_CONV - 1):
                d = FFN_CONV - 1 - k
                rolled = pltpu.roll(gate, d, axis=0)
                head = jnp.where(row < d, pltpu.roll(prev, d, axis=0), rolled[:sub, :])
                shifted = jnp.concatenate([head, rolled[sub:, :]], axis=0)
                conv = conv + cw[k:k + 1, :] * jnp.where(pos >= d, shifted, 0.0)
            act = 0.5 * conv * (1.0 + jnp.tanh(c0 * (conv + 0.044715 * (conv * conv * conv))))
            act_ref[slot, rows, :] = (act * up).astype(BF16)
            prev = gate[FFN_ROWS - sub:, :]
        tail_ref[j] = prev

    def consume(slot):
        o_ref[...] += _dot(act_ref[slot], wd_ref[...])

    @pl.when(first)
    def _():
        tail_ref[...] = jnp.zeros_like(tail_ref)
        produce(0)

    @pl.when((i < nt) & (j == 1))
    def _():
        o_ref[...] = jnp.zeros_like(o_ref)

    for parity in range(2):
        @pl.when(jnp.logical_not(first) & (i < nt) & ((i + j) % 2 == parity))
        def _(parity=parity):
            produce(parity)
            consume(1 - parity)

    @pl.when((i == nt) & (j == 0))
    def _():
        consume((nt * nf - 1) % 2)

    @pl.when(jnp.logical_not(first) & (j == 0))
    def _():
        x1_copy(i - 1).wait()
        o_ref[...] = x1_ref[...] + _rms(o_ref[...], g_ref[...])


def _ffn(h2, x1, wg, wu, wd, cwb, g, seq, tm=1024, tf=512):
    t = h2.shape[0]
    nt = t // tm
    nf = D_FF // tf
    assert nf % 2 == 1
    row_p = lambda i, j: (jnp.minimum(i, nt - 1), 0)
    col_p = lambda i, j: (0, jnp.where(i < nt, j, nf - 1))
    row_c = lambda i, j: (jnp.where(j > 0, jnp.minimum(i, nt - 1), jnp.maximum(i - 1, 0)), 0)
    return pl.pallas_call(
        functools.partial(_ffn_kernel, tm=tm, tf=tf, seq=seq),
        grid=(nt + 1, nf),
        in_specs=[
            pl.BlockSpec((tm, D_MODEL), row_p),
            pl.BlockSpec(memory_space=pl.ANY),
            pl.BlockSpec((D_MODEL, tf), col_p),
            pl.BlockSpec((D_MODEL, tf), col_p),
            pl.BlockSpec((tf, D_MODEL), lambda i, j: (jnp.where(j > 0, j - 1, nf - 1), 0)),
            pl.BlockSpec((FFN_CONV + 1, D_FF), lambda i, j: (0, 0)),
            pl.BlockSpec((1, D_MODEL), lambda i, j: (0, 0)),
        ],
        out_specs=pl.BlockSpec((tm, D_MODEL), row_c),
        out_shape=jax.ShapeDtypeStruct((t, D_MODEL), F32),
        scratch_shapes=[
            pltpu.VMEM((2, tm, tf), BF16),
            pltpu.VMEM((nf, 8, tf), F32),
            pltpu.VMEM((tm, D_MODEL), F32),
            pltpu.SemaphoreType.DMA((1,)),
        ],
        compiler_params=pltpu.CompilerParams(
            dimension_semantics=("arbitrary", "arbitrary"), vmem_limit_bytes=VMEM_LIMIT),
        name="ffn",
    )(h2, x1, wg, wu, wd, cwb, g)


def _swap_half(w):
    half = w.shape[-1] // 2
    return jnp.concatenate([-w[..., half:], w[..., :half]], axis=-1)


def _rope_tables(seq):
    inv = 1.0 / (ROPE_THETA ** (jnp.arange(0, QK_ROPE, 2, dtype=F32) / QK_ROPE))
    ang = jnp.arange(seq, dtype=F32)[:, None] * inv[None, :]
    zeros = jnp.zeros((seq, LANE - QK_ROPE), F32)
    cosp = jnp.concatenate([jnp.cos(ang), jnp.cos(ang), zeros], axis=-1)
    sinp = jnp.concatenate([jnp.sin(ang), jnp.sin(ang), zeros], axis=-1)
    return cosp, sinp


def _layer(x2, batch, seq, mix_pre_g, w_in, q_norm_g, w_uq, kv_norm_g, w_ukv, ssm_conv_w, ssm_conv_b,
           dt_bias, a_log, d_skip, ssm_norm_g, w_out, mix_post_g, ffn_pre_g, w_gate, w_up,
           ffn_conv_w, ffn_conv_b, w_down, ffn_post_g):
    row = lambda v: v.reshape(1, -1).astype(F32)
    pad_lane = lambda v: jnp.pad(v.astype(F32), (0, LANE - v.shape[0])).reshape(1, LANE)

    cuts = [0] + np.cumsum([Q_LORA, KV_LORA, QK_ROPE, SSM_INNER, SSM_CONV_CH, SSM_HEADS]).tolist()
    w_cq, w_ckv, w_kr, w_z, w_xbc, w_dt = [w_in[:, a:b] for a, b in zip(cuts[:-1], cuts[1:])]
    w_in_r = jnp.concatenate(
        [w_cq, w_kr, _swap_half(w_kr), w_dt, jnp.zeros((D_MODEL, LANE - SSM_HEADS), w_in.dtype),
         w_ckv, w_xbc, w_z], axis=-1).astype(BF16)

    wq3 = w_uq.reshape(Q_LORA, MLA_HEADS, QK_NOPE + QK_ROPE)
    wq_rope = wq3[..., QK_NOPE:]
    wq_r = jnp.concatenate([wq3[..., :QK_NOPE], wq_rope, _swap_half(wq_rope)], axis=-1)
    wq_r = wq_r.reshape(Q_LORA, MLA_HEADS * Q_HEAD_PAD).astype(BF16)
    wkv3 = w_ukv.reshape(KV_LORA, MLA_HEADS, QK_NOPE + V_HEAD)
    wk_r = wkv3[..., :QK_NOPE].reshape(KV_LORA, MLA_WIDTH).astype(BF16)
    wvt_r = wkv3[..., QK_NOPE:].reshape(KV_LORA, MLA_WIDTH).T.astype(BF16)

    u = _in_proj(x2, row(mix_pre_g), w_in_r)
    cosp, sinp = _rope_tables(seq)
    q, kn, kr, vt = _mla_proj(u, cosp, sinp, row(q_norm_g), row(kv_norm_g), wq_r, wk_r, wvt_r, seq)
    a_out = _mla_attn(q, kn, kr, vt, batch, seq)
    b_out = _ssd(u, ssm_conv_w.astype(F32), row(ssm_conv_b), pad_lane(dt_bias), pad_lane(a_log),
                 row(jnp.repeat(d_skip, SSM_HEAD_DIM)), row(ssm_norm_g), batch, seq)
    w_out_b = w_out.astype(BF16)
    x1, h2 = _out_proj(a_out, b_out, x2, w_out_b[:MLA_WIDTH], w_out_b[MLA_WIDTH:],
                       row(mix_post_g), row(ffn_pre_g))
    return _ffn(h2, x1, w_gate.astype(BF16), w_up.astype(BF16), w_down.astype(BF16),
                jnp.concatenate([ffn_conv_w.astype(F32), row(ffn_conv_b)], axis=0), row(ffn_post_g), seq)


def kernel(x, mix_pre_g, w_in, q_norm_g, w_uq, kv_norm_g, w_ukv, ssm_conv_w, ssm_conv_b, dt_bias, a_log,
           d_skip, ssm_norm_g, w_out, mix_post_g, ffn_pre_g, w_gate, w_up, ffn_conv_w, ffn_conv_b,
           w_down, ffn_post_g):
    batch, seq, _ = x.shape
    x2 = x.reshape(batch * seq, D_MODEL)
    for l in range(mix_pre_g.shape[0]):
        x2 = _layer(x2, batch, seq, mix_pre_g[l], w_in[l], q_norm_g[l], w_uq[l], kv_norm_g[l], w_ukv[l],
                    ssm_conv_w[l], ssm_conv_b[l], dt_bias[l], a_log[l], d_skip[l], ssm_norm_g[l],
                    w_out[l], mix_post_g[l], ffn_pre_g[l], w_gate[l], w_up[l], ffn_conv_w[l],
                    ffn_conv_b[l], w_down[l], ffn_post_g[l])
    return x2.reshape(batch, seq, D_MODEL)
```

```python
import functools

import jax
import jax.numpy as jnp
import numpy as np
from jax import lax
from jax.experimental import pallas as pl
from jax.experimental.pallas import tpu as pltpu

F32 = jnp.float32
BF16 = jnp.bfloat16

D_MODEL = 2048
CHUNK = 64
EPS = 1e-6

MLA_HEADS = 8
Q_LORA = 768
KV_LORA = 512
QK_NOPE = 128
QK_ROPE = 64
V_HEAD = 128
ROPE_THETA = 10000.0
MLA_WIDTH = MLA_HEADS * V_HEAD
Q_HEAD_PAD = 256

SSM_HEADS = 16
SSM_HEAD_DIM = 64
SSM_INNER = SSM_HEADS * SSM_HEAD_DIM
SSM_GROUPS = 2
SSM_STATE = 128
SSM_CONV = 4
SSM_CONV_CH = SSM_INNER + 2 * SSM_GROUPS * SSM_STATE
GROUP_WIDTH = SSM_INNER // SSM_GROUPS

D_FF = 5632
FFN_CONV = 3
FFN_ROWS = 256

LANE = 128
U_COLS = 4096
U_CQ = 0
U_KR = Q_LORA
U_DT = U_KR + LANE
U_CKV = U_DT + LANE
U_XBC = U_CKV + KV_LORA
U_Z = U_XBC + SSM_CONV_CH

VMEM_LIMIT = 56 * 1024 * 1024


def _rms(x, g):
    return x * lax.rsqrt(jnp.mean(x * x, axis=-1, keepdims=True) + EPS) * g


def _sigmoid(x):
    return 1.0 / (1.0 + jnp.exp(-x))


def _dot(a, b):
    return jnp.dot(a, b, preferred_element_type=F32)


def _dot_nt(a, b):
    return lax.dot_general(a, b, (((1,), (1,)), ((), ())), preferred_element_type=F32)


def _dot_tn(a, b):
    return lax.dot_general(a, b, (((0,), (0,)), ((), ())), preferred_element_type=F32)


def _split3(x, lane):
    xm = jnp.where(lane < SSM_HEADS, x, 0.0)
    hi = xm.astype(BF16).astype(F32)
    r1 = xm - hi
    mid = r1.astype(BF16).astype(F32)
    lo = (r1 - mid).astype(BF16).astype(F32)
    packed = hi + pltpu.roll(mid, SSM_HEADS, axis=1) + pltpu.roll(lo, 2 * SSM_HEADS, axis=1)
    return packed.astype(BF16)


def _in_proj_kernel(x_ref, g_ref, w_ref, o_ref):
    x = x_ref[...]
    r = lax.rsqrt(jnp.mean(x * x, axis=-1, keepdims=True) + EPS)
    o_ref[...] = _dot((x * g_ref[...]).astype(BF16), w_ref[...]) * r


def _in_proj(x2, g, w, tm=512):
    t = x2.shape[0]
    return pl.pallas_call(
        _in_proj_kernel,
        grid=(t // tm,),
        in_specs=[
            pl.BlockSpec((tm, D_MODEL), lambda i: (i, 0)),
            pl.BlockSpec((1, D_MODEL), lambda i: (0, 0)),
            pl.BlockSpec((D_MODEL, U_COLS), lambda i: (0, 0), pipeline_mode=pl.Buffered(1)),
        ],
        out_specs=pl.BlockSpec((tm, U_COLS), lambda i: (i, 0)),
        out_shape=jax.ShapeDtypeStruct((t, U_COLS), F32),
        compiler_params=pltpu.CompilerParams(
            dimension_semantics=("parallel",), vmem_limit_bytes=VMEM_LIMIT),
        name="in_proj",
    )(x2, g, w)


def _rope_pair(blk, cosp, sinp):
    return blk * cosp + pltpu.roll(blk, QK_ROPE, axis=1) * sinp


def _mla_proj_kernel(cq_ref, ckv_ref, kr_ref, cos_ref, sin_ref, cost_ref, sint_ref, qg_ref, kvg_ref,
                     wqt_ref, wk_ref, wvt_ref, qt_ref, kn_ref, krot_ref, vt_ref, *, scale):
    cosp = cos_ref[...]
    sinp = sin_ref[...]
    cost = cost_ref[...]
    sint = sint_ref[...]
    qall = _dot_nt(wqt_ref[...], _rms(cq_ref[...], qg_ref[...]).astype(BF16))
    for h in range(MLA_HEADS):
        base = h * Q_HEAD_PAD
        rope = base + QK_NOPE
        qt_ref[base:rope, :] = (qall[base:rope, :] * scale).astype(BF16)
        rot = qall[rope:rope + QK_ROPE, :] * cost + qall[rope + QK_ROPE:base + Q_HEAD_PAD, :] * sint
        qt_ref[rope:rope + QK_ROPE, :] = (rot * scale).astype(BF16)
        qt_ref[rope + QK_ROPE:base + Q_HEAD_PAD, :] = jnp.zeros((QK_ROPE, rot.shape[1]), BF16)
    ckv = _rms(ckv_ref[...], kvg_ref[...]).astype(BF16)
    kn_ref[...] = _dot(ckv, wk_ref[...]).astype(BF16)
    vt_ref[...] = _dot_nt(wvt_ref[...], ckv).astype(BF16)
    krot_ref[...] = _rope_pair(kr_ref[...], cosp, sinp).astype(BF16)


def _mla_proj(u, cosp, sinp, cost, sint, qg, kvg, wqt, wk, wvt, seq, tm=512):
    t = u.shape[0]
    nseq = seq // tm
    scale = float((QK_NOPE + QK_ROPE) ** -0.5 * np.log2(np.e))
    row = lambda i: (i, 0)
    const = lambda i: (0, 0)
    return pl.pallas_call(
        functools.partial(_mla_proj_kernel, scale=scale),
        grid=(t // tm,),
        in_specs=[
            pl.BlockSpec((tm, Q_LORA), lambda i: (i, U_CQ // Q_LORA)),
            pl.BlockSpec((tm, KV_LORA), lambda i: (i, U_CKV // KV_LORA)),
            pl.BlockSpec((tm, LANE), lambda i: (i, U_KR // LANE)),
            pl.BlockSpec((tm, LANE), lambda i: (i % nseq, 0)),
            pl.BlockSpec((tm, LANE), lambda i: (i % nseq, 0)),
            pl.BlockSpec((QK_ROPE, tm), lambda i: (0, i % nseq)),
            pl.BlockSpec((QK_ROPE, tm), lambda i: (0, i % nseq)),
            pl.BlockSpec((1, Q_LORA), const),
            pl.BlockSpec((1, KV_LORA), const),
            pl.BlockSpec((MLA_HEADS * Q_HEAD_PAD, Q_LORA), const),
            pl.BlockSpec((KV_LORA, MLA_WIDTH), const),
            pl.BlockSpec((MLA_WIDTH, KV_LORA), const),
        ],
        out_specs=[
            pl.BlockSpec((MLA_HEADS * Q_HEAD_PAD, tm), lambda i: (0, i)),
            pl.BlockSpec((tm, MLA_WIDTH), row),
            pl.BlockSpec((tm, LANE), row),
            pl.BlockSpec((MLA_WIDTH, tm), lambda i: (0, i)),
        ],
        out_shape=[
            jax.ShapeDtypeStruct((MLA_HEADS * Q_HEAD_PAD, t), BF16),
            jax.ShapeDtypeStruct((t, MLA_WIDTH), BF16),
            jax.ShapeDtypeStruct((t, LANE), BF16),
            jax.ShapeDtypeStruct((MLA_WIDTH, t), BF16),
        ],
        compiler_params=pltpu.CompilerParams(
            dimension_semantics=("parallel",), vmem_limit_bytes=VMEM_LIMIT),
        name="mla_proj",
    )(u, u, u, cosp, sinp, cost, sint, qg, kvg, wqt, wk, wvt)


ONES_ROWS = 16


def _attn_kernel(qt_ref, kn_ref, kr_ref, vt_ref, o_ref, s_ref, mrun_ref, acc_ref, *, tq):
    qi = pl.program_id(1)
    sublanes = mrun_ref.shape[1]
    krow = lax.broadcasted_iota(jnp.int32, (tq, tq), 0) // CHUNK
    qcol = lax.broadcasted_iota(jnp.int32, (tq, tq), 1) // CHUNK
    diag_mask = krow <= qcol
    ones = jnp.ones((ONES_ROWS, tq), BF16)

    mrun_ref[...] = jnp.full(mrun_ref.shape, -1e30, F32)
    acc_ref[...] = jnp.zeros_like(acc_ref)

    def scores(j, masked):
        ks = pl.multiple_of(j * tq, tq)
        kr = kr_ref[pl.ds(ks, tq), :]
        for h in range(MLA_HEADS):
            qt = qt_ref[h * Q_HEAD_PAD:(h + 1) * Q_HEAD_PAD, :]
            k = jnp.concatenate([kn_ref[pl.ds(ks, tq), h * QK_NOPE:(h + 1) * QK_NOPE], kr], axis=-1)
            s = _dot(k, qt)
            if masked:
                s = jnp.where(diag_mask, s, -1e30)
            s_ref[h, j] = s
            tile_max = jnp.max(s.reshape(tq // sublanes, sublanes, tq), axis=0)
            mrun_ref[h] = jnp.maximum(mrun_ref[h], tile_max)

    def scores_pair(p, carry):
        scores(2 * p, masked=False)
        scores(2 * p + 1, masked=False)
        return carry

    lax.fori_loop(0, lax.shift_right_logical(qi, 1), scores_pair, 0)

    @pl.when(qi % 2 == 1)
    def _():
        scores(qi - 1, masked=False)

    scores(qi, masked=True)

    m = [jnp.max(mrun_ref[h], axis=0, keepdims=True) for h in range(MLA_HEADS)]

    def accumulate(j):
        ks = pl.multiple_of(j * tq, tq)
        for h in range(MLA_HEADS):
            p = jnp.exp2(s_ref[h, j] - m[h]).astype(BF16)
            vt = jnp.concatenate([vt_ref[h * V_HEAD:(h + 1) * V_HEAD, pl.ds(ks, tq)], ones], axis=0)
            acc_ref[h] += _dot(vt, p)

    def accumulate_pair(p, carry):
        accumulate(2 * p)
        accumulate(2 * p + 1)
        return carry

    lax.fori_loop(0, lax.shift_right_logical(qi + 1, 1), accumulate_pair, 0)

    @pl.when(qi % 2 == 0)
    def _():
        accumulate(qi)

    for h in range(MLA_HEADS):
        acc = acc_ref[h]
        out_t = acc[:V_HEAD, :] / acc[V_HEAD:V_HEAD + 1, :]
        o_ref[:, h * V_HEAD:(h + 1) * V_HEAD] = out_t.T.astype(BF16)


def _mla_attn(qt, kn, kr, vt, batch, seq, tq=256):
    t = kn.shape[0]
    nq = seq // tq
    return pl.pallas_call(
        functools.partial(_attn_kernel, tq=tq),
        grid=(batch, nq),
        in_specs=[
            pl.BlockSpec((MLA_HEADS * Q_HEAD_PAD, tq), lambda b, i: (0, b * nq + i)),
            pl.BlockSpec((seq, MLA_WIDTH), lambda b, i: (b, 0)),
            pl.BlockSpec((seq, LANE), lambda b, i: (b, 0)),
            pl.BlockSpec((MLA_WIDTH, seq), lambda b, i: (0, b)),
        ],
        out_specs=pl.BlockSpec((tq, MLA_WIDTH), lambda b, i: (b * nq + i, 0)),
        out_shape=jax.ShapeDtypeStruct((t, MLA_WIDTH), BF16),
        scratch_shapes=[
            pltpu.VMEM((MLA_HEADS, nq, tq, tq), F32),
            pltpu.VMEM((MLA_HEADS, 8, tq), F32),
            pltpu.VMEM((MLA_HEADS, V_HEAD + ONES_ROWS, tq), F32),
        ],
        compiler_params=pltpu.CompilerParams(
            dimension_semantics=("parallel", "arbitrary"), vmem_limit_bytes=VMEM_LIMIT),
        name="mla_attn",
    )(qt, kn, kr, vt)


def _ssd_kernel(z_ref, xbc_ref, prev_ref, dt_ref, cw_ref, cb_ref, dtb_ref, alog_ref, dexp_ref, ng_ref,
                o_ref, st_ref, xc_ref, dte_ref, acse_ref, wb_ref, *, ts):
    s_idx = pl.program_id(1)
    halo = prev_ref.shape[0]

    @pl.when(s_idx == 0)
    def _():
        st_ref[...] = jnp.zeros_like(st_ref)

    sub = lax.broadcasted_iota(jnp.int32, (halo, SSM_CONV_CH), 0)
    for k in range(SSM_CONV):
        wb_ref[k] = jnp.broadcast_to(cw_ref[k:k + 1, :], (halo, SSM_CONV_CH))
    wb_ref[SSM_CONV] = jnp.broadcast_to(cb_ref[...], (halo, SSM_CONV_CH))

    def conv_tile(i, prev):
        r = pl.multiple_of(i * halo, halo)
        cur = xbc_ref[pl.ds(r, halo), :]
        conv = wb_ref[SSM_CONV] + wb_ref[SSM_CONV - 1] * cur
        for d in range(1, SSM_CONV):
            shifted = jnp.where(sub < d, pltpu.roll(prev, d, axis=0), pltpu.roll(cur, d, axis=0))
            conv = conv + wb_ref[SSM_CONV - 1 - d] * shifted
        xc_ref[pl.ds(r, halo), :] = conv * _sigmoid(conv)
        return cur

    lax.fori_loop(0, ts // halo, conv_tile, jnp.where(s_idx > 0, prev_ref[...], 0.0), unroll=True)

    raw = dt_ref[...] + dtb_ref[...]
    dt = jnp.maximum(raw, 0.0) + jnp.log1p(jnp.exp(-jnp.abs(raw)))
    a = dt * (-jnp.exp(alog_ref[...]))
    lane = lax.broadcasted_iota(jnp.int32, (ts, LANE), 1)
    ri = lax.broadcasted_iota(jnp.int32, (ts, ts), 0)
    ci = lax.broadcasted_iota(jnp.int32, (ts, ts), 1)
    tri = jnp.where((ri // CHUNK == ci // CHUNK) & (ci <= ri), 1.0, 0.0).astype(BF16)
    c3 = _dot(tri, _split3(a, lane))
    acs = c3 + pltpu.roll(c3, LANE - SSM_HEADS, axis=1) + pltpu.roll(c3, LANE - 2 * SSM_HEADS, axis=1)
    er = lax.broadcasted_iota(jnp.int32, (LANE, SSM_INNER), 0)
    ec = lax.broadcasted_iota(jnp.int32, (LANE, SSM_INNER), 1) // SSM_HEAD_DIM
    expand = jnp.where((er % SSM_HEADS == ec) & (er < 3 * SSM_HEADS), 1.0, 0.0).astype(BF16)
    log2e = float(np.log2(np.e))
    both = _dot(jnp.concatenate([_split3(dt, lane), _split3(acs * log2e, lane)], axis=0), expand)
    dte_ref[...] = both[:ts]
    acse_ref[...] = both[ts:]

    li = lax.broadcasted_iota(jnp.int32, (CHUNK, SSM_INNER), 0)
    si = lax.broadcasted_iota(jnp.int32, (CHUNK, SSM_INNER), 1) % SSM_HEAD_DIM
    quad = 4 * SSM_HEAD_DIM
    bd_r = lax.broadcasted_iota(jnp.int32, (quad, quad), 0) // SSM_HEAD_DIM
    bd_c = lax.broadcasted_iota(jnp.int32, (quad, quad), 1) // SSM_HEAD_DIM
    bd_mask = bd_r == bd_c
    dexp = dexp_ref[...]
    ng = ng_ref[...]

    def chunk_body(c, carry):
        r0 = pl.multiple_of(c * CHUNK, CHUNK)
        rows = pl.ds(r0, CHUNK)
        xs = xc_ref[rows, 0:SSM_INNER]
        acx = acse_ref[rows, :]
        last = acx[CHUNK - 1:CHUNK, :]
        xdt = xs * dte_ref[rows, :]
        rv = jnp.sum(jnp.where(li == si, acx, 0.0), axis=0, keepdims=True)
        lmat = jnp.exp2(jnp.where(li >= si, acx - rv, -jnp.inf))
        xdec = (xdt * jnp.exp2(last - acx)).astype(BF16)
        xdt_b = xdt.astype(BF16)
        eacx = jnp.exp2(acx)
        elast = jnp.exp2(last)
        ys = []
        for g in range(SSM_GROUPS):
            gl = slice(g * GROUP_WIDTH, (g + 1) * GROUP_WIDTH)
            b_g = xc_ref[rows, SSM_INNER + g * SSM_STATE:SSM_INNER + (g + 1) * SSM_STATE].astype(BF16)
            c0 = SSM_INNER + SSM_GROUPS * SSM_STATE + g * SSM_STATE
            c_g = xc_ref[rows, c0:c0 + SSM_STATE].astype(BF16)
            gt = _dot_nt(c_g, jnp.concatenate([b_g] * 4, axis=0))
            st_g = st_ref[:, gl]
            y_off = _dot(c_g, st_g.astype(BF16)) * eacx[:, gl]
            yd = []
            for qq in range(GROUP_WIDTH // quad):
                sl = slice(g * GROUP_WIDTH + qq * quad, g * GROUP_WIDTH + (qq + 1) * quad)
                m_q = (gt * lmat[:, sl]).astype(BF16)
                x_q = jnp.concatenate([xdt_b[:, sl]] * 4, axis=0)
                x_q = jnp.where(bd_mask, x_q, jnp.zeros_like(x_q))
                yd.append(_dot(m_q, x_q))
            ys.append(jnp.concatenate(yd, axis=-1) + y_off)
            st_ref[:, gl] = st_g * elast[:, gl] + _dot_tn(b_g, xdec[:, gl])
        y = jnp.concatenate(ys, axis=-1) + xs * dexp
        z = z_ref[rows, :]
        y = y * (z * _sigmoid(z))
        outs = []
        for g in range(SSM_GROUPS):
            yg = y[:, g * GROUP_WIDTH:(g + 1) * GROUP_WIDTH]
            outs.append(yg * lax.rsqrt(jnp.mean(yg * yg, axis=-1, keepdims=True) + EPS))
        o_ref[rows, :] = (jnp.concatenate(outs, axis=-1) * ng).astype(BF16)
        return carry

    lax.fori_loop(0, ts // CHUNK, chunk_body, 0, unroll=True)


def _ssd(u, cw, cb, dtb, alog, dexp, ng, batch, seq, ts=256, halo=8):
    t = u.shape[0]
    ns = seq // ts
    const = lambda b, s: (0, 0)
    rowblk = lambda b, s: b * ns + s
    return pl.pallas_call(
        functools.partial(_ssd_kernel, ts=ts),
        grid=(batch, ns),
        in_specs=[
            pl.BlockSpec((ts, SSM_INNER), lambda b, s: (rowblk(b, s), U_Z // SSM_INNER)),
            pl.BlockSpec((ts, SSM_CONV_CH), lambda b, s: (rowblk(b, s), U_XBC // SSM_CONV_CH)),
            pl.BlockSpec((halo, SSM_CONV_CH),
                         lambda b, s: (jnp.maximum(rowblk(b, s) * (ts // halo) - 1, 0), U_XBC // SSM_CONV_CH)),
            pl.BlockSpec((ts, LANE), lambda b, s: (rowblk(b, s), U_DT // LANE)),
            pl.BlockSpec((SSM_CONV, SSM_CONV_CH), const),
            pl.BlockSpec((1, SSM_CONV_CH), const),
            pl.BlockSpec((1, LANE), const),
            pl.BlockSpec((1, LANE), const),
            pl.BlockSpec((1, SSM_INNER), const),
            pl.BlockSpec((1, SSM_INNER), const),
        ],
        out_specs=pl.BlockSpec((ts, SSM_INNER), lambda b, s: (rowblk(b, s), 0)),
        out_shape=jax.ShapeDtypeStruct((t, SSM_INNER), BF16),
        scratch_shapes=[
            pltpu.VMEM((SSM_STATE, SSM_INNER), F32),
            pltpu.VMEM((ts, SSM_CONV_CH), F32),
            pltpu.VMEM((ts, SSM_INNER), F32),
            pltpu.VMEM((ts, SSM_INNER), F32),
            pltpu.VMEM((SSM_CONV + 1, halo, SSM_CONV_CH), F32),
        ],
        compiler_params=pltpu.CompilerParams(
            dimension_semantics=("parallel", "arbitrary"), vmem_limit_bytes=VMEM_LIMIT),
        name="ssd",
    )(u, u, u, u, cw, cb, dtb, alog, dexp, ng)


def _out_proj_kernel(a_ref, b_ref, x_ref, wa_ref, wb_ref, g1_ref, g2_ref, x1_ref, h2_ref, *, piece):
    for r in range(0, a_ref.shape[0], piece):
        rows = slice(r, r + piece)
        mix = _dot(a_ref[rows, :], wa_ref[...]) + _dot(b_ref[rows, :], wb_ref[...])
        x1 = x_ref[rows, :] + _rms(mix, g1_ref[...])
        x1_ref[rows, :] = x1
        h2_ref[rows, :] = _rms(x1, g2_ref[...]).astype(BF16)


def _out_proj(a_out, b_out, x2, wa, wb, g1, g2, tm=512, piece=256):
    t = x2.shape[0]
    row = lambda i: (i, 0)
    const = lambda i: (0, 0)
    return pl.pallas_call(
        functools.partial(_out_proj_kernel, piece=piece),
        grid=(t // tm,),
        in_specs=[
            pl.BlockSpec((tm, MLA_WIDTH), row),
            pl.BlockSpec((tm, SSM_INNER), row),
            pl.BlockSpec((tm, D_MODEL), row),
            pl.BlockSpec((MLA_WIDTH, D_MODEL), const),
            pl.BlockSpec((SSM_INNER, D_MODEL), const),
            pl.BlockSpec((1, D_MODEL), const),
            pl.BlockSpec((1, D_MODEL), const),
        ],
        out_specs=[pl.BlockSpec((tm, D_MODEL), row), pl.BlockSpec((tm, D_MODEL), row)],
        out_shape=[jax.ShapeDtypeStruct((t, D_MODEL), F32), jax.ShapeDtypeStruct((t, D_MODEL), BF16)],
        compiler_params=pltpu.CompilerParams(
            dimension_semantics=("parallel",), vmem_limit_bytes=VMEM_LIMIT),
        name="out_proj",
    )(a_out, b_out, x2, wa, wb, g1, g2)


def _ffn_kernel(h_ref, x1_hbm, wg_ref, wu_ref, wd_ref, cw_ref, g_ref, o_ref,
                act_ref, tail_ref, x1_ref, x1_sem, *, tm, tf, seq):
    i = pl.program_id(0)
    j = pl.program_id(1)
    nt = pl.num_programs(0) - 1
    nf = tail_ref.shape[0]
    sub = tail_ref.shape[1]
    first = (i == 0) & (j == 0)
    cols = pl.ds(pl.multiple_of(j * tf, tf), tf)

    def x1_copy(tile):
        rows = pl.ds(pl.multiple_of(tile * tm, tm), tm)
        return pltpu.make_async_copy(x1_hbm.at[rows, :], x1_ref, x1_sem.at[0])

    @pl.when((i < nt) & (j == nf - 1))
    def _():
        x1_copy(i).start()

    def produce(slot):
        cw = cw_ref[:, cols]
        prev = tail_ref[j]
        row = lax.broadcasted_iota(jnp.int32, prev.shape, 0)
        c0 = float(np.sqrt(2.0 / np.pi))
        for r in range(0, tm, FFN_ROWS):
            rows = slice(r, r + FFN_ROWS)
            gate = _dot(h_ref[rows, :], wg_ref[...])
            up = _dot(h_ref[rows, :], wu_ref[...])
            pos = (i * tm + r) % seq + lax.broadcasted_iota(jnp.int32, (FFN_ROWS, 1), 0)
            conv = cw[FFN_CONV:FFN_CONV + 1, :] + cw[FFN_CONV - 1:FFN_CONV, :] * gate
            for k in range(FFN_CONV - 1):
                d = FFN_CONV - 1 - k
                rolled = pltpu.roll(gate, d, axis=0)
                head = jnp.where(row < d, pltpu.roll(prev, d, axis=0), rolled[:sub, :])
                shifted = jnp.concatenate([head, rolled[sub:, :]], axis=0)
                conv = conv + cw[k:k + 1, :] * jnp.where(pos >= d, shifted, 0.0)
            act = 0.5 * conv * (1.0 + jnp.tanh(c0 * (conv + 0.044715 * (conv * conv * conv))))
            act_ref[slot, rows, :] = (act * up).astype(BF16)
            prev = gate[FFN_ROWS - sub:, :]
        tail_ref[j] = prev

    def consume(slot):
        o_ref[...] += _dot(act_ref[slot], wd_ref[...])

    @pl.when(first)
    def _():
        tail_ref[...] = jnp.zeros_like(tail_ref)
        produce(0)

    @pl.when((i < nt) & (j == 1))
    def _():
        o_ref[...] = jnp.zeros_like(o_ref)

    for parity in range(2):
        @pl.when(jnp.logical_not(first) & (i < nt) & ((i + j) % 2 == parity))
        def _(parity=parity):
            produce(parity)
            consume(1 - parity)

    @pl.when((i == nt) & (j == 0))
    def _():
        consume((nt * nf - 1) % 2)

    @pl.when(jnp.logical_not(first) & (j == 0))
    def _():
        x1_copy(i - 1).wait()
        o_ref[...] = x1_ref[...] + _rms(o_ref[...], g_ref[...])


def _ffn(h2, x1, wg, wu, wd, cwb, g, seq, tm=1024, tf=512):
    t = h2.shape[0]
    nt = t // tm
    nf = D_FF // tf
    assert nf % 2 == 1
    row_p = lambda i, j: (jnp.minimum(i, nt - 1), 0)
    col_p = lambda i, j: (0, jnp.where(i < nt, j, nf - 1))
    row_c = lambda i, j: (jnp.where(j > 0, jnp.minimum(i, nt - 1), jnp.maximum(i - 1, 0)), 0)
    return pl.pallas_call(
        functools.partial(_ffn_kernel, tm=tm, tf=tf, seq=seq),
        grid=(nt + 1, nf),
        in_specs=[
            pl.BlockSpec((tm, D_MODEL), row_p),
            pl.BlockSpec(memory_space=pl.ANY),
            pl.BlockSpec((D_MODEL, tf), col_p),
            pl.BlockSpec((D_MODEL, tf), col_p),
            pl.BlockSpec((tf, D_MODEL), lambda i, j: (jnp.where(j > 0, j - 1, nf - 1), 0)),
            pl.BlockSpec((FFN_CONV + 1, D_FF), lambda i, j: (0, 0)),
            pl.BlockSpec((1, D_MODEL), lambda i, j: (0, 0)),
        ],
        out_specs=pl.BlockSpec((tm, D_MODEL), row_c),
        out_shape=jax.ShapeDtypeStruct((t, D_MODEL), F32),
        scratch_shapes=[
            pltpu.VMEM((2, tm, tf), BF16),
            pltpu.VMEM((nf, 8, tf), F32),
            pltpu.VMEM((tm, D_MODEL), F32),
            pltpu.SemaphoreType.DMA((1,)),
        ],
        compiler_params=pltpu.CompilerParams(
            dimension_semantics=("arbitrary", "arbitrary"), vmem_limit_bytes=VMEM_LIMIT),
        name="ffn",
    )(h2, x1, wg, wu, wd, cwb, g)


def _swap_half(w):
    half = w.shape[-1] // 2
    return jnp.concatenate([-w[..., half:], w[..., :half]], axis=-1)


def _rope_tables(seq):
    inv = 1.0 / (ROPE_THETA ** (jnp.arange(0, QK_ROPE, 2, dtype=F32) / QK_ROPE))
    ang = jnp.arange(seq, dtype=F32)[:, None] * inv[None, :]
    zeros = jnp.zeros((seq, LANE - QK_ROPE), F32)
    cos2 = jnp.concatenate([jnp.cos(ang), jnp.cos(ang)], axis=-1)
    sin2 = jnp.concatenate([jnp.sin(ang), jnp.sin(ang)], axis=-1)
    cosp = jnp.concatenate([cos2, zeros], axis=-1)
    sinp = jnp.concatenate([sin2, zeros], axis=-1)
    return cosp, sinp, cos2.T, sin2.T


def _layer(x2, batch, seq, mix_pre_g, w_in, q_norm_g, w_uq, kv_norm_g, w_ukv, ssm_conv_w, ssm_conv_b,
           dt_bias, a_log, d_skip, ssm_norm_g, w_out, mix_post_g, ffn_pre_g, w_gate, w_up,
           ffn_conv_w, ffn_conv_b, w_down, ffn_post_g):
    row = lambda v: v.reshape(1, -1).astype(F32)
    pad_lane = lambda v: jnp.pad(v.astype(F32), (0, LANE - v.shape[0])).reshape(1, LANE)

    cuts = [0] + np.cumsum([Q_LORA, KV_LORA, QK_ROPE, SSM_INNER, SSM_CONV_CH, SSM_HEADS]).tolist()
    w_cq, w_ckv, w_kr, w_z, w_xbc, w_dt = [w_in[:, a:b] for a, b in zip(cuts[:-1], cuts[1:])]
    w_in_r = jnp.concatenate(
        [w_cq, w_kr, _swap_half(w_kr), w_dt, jnp.zeros((D_MODEL, LANE - SSM_HEADS), w_in.dtype),
         w_ckv, w_xbc, w_z], axis=-1).astype(BF16)

    wq3 = w_uq.reshape(Q_LORA, MLA_HEADS, QK_NOPE + QK_ROPE)
    wq_rope = wq3[..., QK_NOPE:]
    wq_r = jnp.concatenate([wq3[..., :QK_NOPE], wq_rope, _swap_half(wq_rope)], axis=-1)
    wqt_r = wq_r.reshape(Q_LORA, MLA_HEADS * Q_HEAD_PAD).T.astype(BF16)
    wkv3 = w_ukv.reshape(KV_LORA, MLA_HEADS, QK_NOPE + V_HEAD)
    wk_r = wkv3[..., :QK_NOPE].reshape(KV_LORA, MLA_WIDTH).astype(BF16)
    wvt_r = wkv3[..., QK_NOPE:].reshape(KV_LORA, MLA_WIDTH).T.astype(BF16)

    u = _in_proj(x2, row(mix_pre_g), w_in_r)
    cosp, sinp, cost, sint = _rope_tables(seq)
    qt, kn, kr, vt = _mla_proj(u, cosp, sinp, cost, sint, row(q_norm_g), row(kv_norm_g), wqt_r, wk_r, wvt_r, seq)
    a_out = _mla_attn(qt, kn, kr, vt, batch, seq)
    b_out = _ssd(u, ssm_conv_w.astype(F32), row(ssm_conv_b), pad_lane(dt_bias), pad_lane(a_log),
                 row(jnp.repeat(d_skip, SSM_HEAD_DIM)), row(ssm_norm_g), batch, seq)
    w_out_b = w_out.astype(BF16)
    x1, h2 = _out_proj(a_out, b_out, x2, w_out_b[:MLA_WIDTH], w_out_b[MLA_WIDTH:],
                       row(mix_post_g), row(ffn_pre_g))
    return _ffn(h2, x1, w_gate.astype(BF16), w_up.astype(BF16), w_down.astype(BF16),
                jnp.concatenate([ffn_conv_w.astype(F32), row(ffn_conv_b)], axis=0), row(ffn_post_g), seq)


def kernel(x, mix_pre_g, w_in, q_norm_g, w_uq, kv_norm_g, w_ukv, ssm_conv_w, ssm_conv_b, dt_bias, a_log,
           d_skip, ssm_norm_g, w_out, mix_post_g, ffn_pre_g, w_gate, w_up, ffn_conv_w, ffn_conv_b,
           w_down, ffn_post_g):
    batch, seq, _ = x.shape
    x2 = x.reshape(batch * seq, D_MODEL)
    for l in range(mix_pre_g.shape[0]):
        x2 = _layer(x2, batch, seq, mix_pre_g[l], w_in[l], q_norm_g[l], w_uq[l], kv_norm_g[l], w_ukv[l],
                    ssm_conv_w[l], ssm_conv_b[l], dt_bias[l], a_log[l], d_skip[l], ssm_norm_g[l],
                    w_out[l], mix_post_g[l], ffn_pre_g[l], w_gate[l], w_up[l], ffn_conv_w[l],
                    ffn_conv_b[l], w_down[l], ffn_post_g[l])
    return x2.reshape(batch, seq, D_MODEL)
```

```python
import functools

import jax
import jax.numpy as jnp
import numpy as np
from jax import lax
from jax.experimental import pallas as pl
from jax.experimental.pallas import tpu as pltpu

F32 = jnp.float32
BF16 = jnp.bfloat16

D_MODEL = 2048
CHUNK = 64
EPS = 1e-6

MLA_HEADS = 8
Q_LORA = 768
KV_LORA = 512
QK_NOPE = 128
QK_ROPE = 64
V_HEAD = 128
ROPE_THETA = 10000.0
MLA_WIDTH = MLA_HEADS * V_HEAD
Q_HEAD_PAD = 256

SSM_HEADS = 16
SSM_HEAD_DIM = 64
SSM_INNER = SSM_HEADS * SSM_HEAD_DIM
SSM_GROUPS = 2
SSM_STATE = 128
SSM_CONV = 4
SSM_CONV_CH = SSM_INNER + 2 * SSM_GROUPS * SSM_STATE
GROUP_WIDTH = SSM_INNER // SSM_GROUPS

D_FF = 5632
FFN_CONV = 3
FFN_ROWS = 256

LANE = 128
U_COLS = 4096
U_CQ = 0
U_KR = Q_LORA
U_DT = U_KR + LANE
U_CKV = U_DT + LANE
U_XBC = U_CKV + KV_LORA
U_Z = U_XBC + SSM_CONV_CH

VMEM_LIMIT = 56 * 1024 * 1024


def _rms(x, g):
    return x * lax.rsqrt(jnp.mean(x * x, axis=-1, keepdims=True) + EPS) * g


def _sigmoid(x):
    return 1.0 / (1.0 + jnp.exp(-x))


def _dot(a, b):
    return jnp.dot(a, b, preferred_element_type=F32)


def _dot_nt(a, b):
    return lax.dot_general(a, b, (((1,), (1,)), ((), ())), preferred_element_type=F32)


def _dot_tn(a, b):
    return lax.dot_general(a, b, (((0,), (0,)), ((), ())), preferred_element_type=F32)


def _split3(x, lane):
    xm = jnp.where(lane < SSM_HEADS, x, 0.0)
    hi = xm.astype(BF16).astype(F32)
    r1 = xm - hi
    mid = r1.astype(BF16).astype(F32)
    lo = (r1 - mid).astype(BF16).astype(F32)
    packed = hi + pltpu.roll(mid, SSM_HEADS, axis=1) + pltpu.roll(lo, 2 * SSM_HEADS, axis=1)
    return packed.astype(BF16)


def _in_proj_kernel(x_ref, g_ref, w_ref, o_ref):
    x = x_ref[...]
    r = lax.rsqrt(jnp.mean(x * x, axis=-1, keepdims=True) + EPS)
    o_ref[...] = _dot((x * g_ref[...]).astype(BF16), w_ref[...]) * r


def _in_proj(x2, g, w, tm=512):
    t = x2.shape[0]
    return pl.pallas_call(
        _in_proj_kernel,
        grid=(t // tm,),
        in_specs=[
            pl.BlockSpec((tm, D_MODEL), lambda i: (i, 0)),
            pl.BlockSpec((1, D_MODEL), lambda i: (0, 0)),
            pl.BlockSpec((D_MODEL, U_COLS), lambda i: (0, 0), pipeline_mode=pl.Buffered(1)),
        ],
        out_specs=pl.BlockSpec((tm, U_COLS), lambda i: (i, 0)),
        out_shape=jax.ShapeDtypeStruct((t, U_COLS), F32),
        compiler_params=pltpu.CompilerParams(
            dimension_semantics=("parallel",), vmem_limit_bytes=VMEM_LIMIT),
        name="in_proj",
    )(x2, g, w)


def _regroup_kernel(w_ref, o_ref):
    src = np.cumsum([0, Q_LORA, KV_LORA, QK_ROPE, SSM_INNER, SSM_CONV_CH]).tolist()
    s_cq, s_ckv, s_kr, s_z, s_xbc, s_dt = src
    for dst, lo, width in ((U_CQ, s_cq, Q_LORA), (U_CKV, s_ckv, KV_LORA), (U_XBC, s_xbc, SSM_CONV_CH),
                           (U_Z, s_z, SSM_INNER), (U_KR, s_kr, QK_ROPE), (U_DT, s_dt, SSM_HEADS)):
        o_ref[:, dst:dst + width] = w_ref[:, lo:lo + width].astype(BF16)
    o_ref[:, U_KR + QK_ROPE:U_DT] = _swap_half(w_ref[:, s_kr:s_kr + QK_ROPE]).astype(BF16)
    o_ref[:, U_DT + SSM_HEADS:U_CKV] = jnp.zeros((o_ref.shape[0], LANE - SSM_HEADS), BF16)


def _regroup_w_in(w_in, tr=256):
    return pl.pallas_call(
        _regroup_kernel,
        grid=(D_MODEL // tr,),
        in_specs=[pl.BlockSpec((tr, w_in.shape[1]), lambda i: (i, 0))],
        out_specs=pl.BlockSpec((tr, U_COLS), lambda i: (i, 0)),
        out_shape=jax.ShapeDtypeStruct((D_MODEL, U_COLS), BF16),
        compiler_params=pltpu.CompilerParams(
            dimension_semantics=("parallel",), vmem_limit_bytes=VMEM_LIMIT),
        name="regroup_w_in",
    )(w_in)


def _rope_pair(blk, cosp, sinp):
    return blk * cosp + pltpu.roll(blk, QK_ROPE, axis=1) * sinp


def _mla_proj_kernel(cq_ref, ckv_ref, kr_ref, cos_ref, sin_ref, cost_ref, sint_ref, qg_ref, kvg_ref,
                     wqt_ref, wk_ref, wvt_ref, qt_ref, kn_ref, krot_ref, vt_ref, *, scale):
    cosp = cos_ref[...]
    sinp = sin_ref[...]
    cost = cost_ref[...]
    sint = sint_ref[...]
    qall = _dot_nt(wqt_ref[...], _rms(cq_ref[...], qg_ref[...]).astype(BF16))
    for h in range(MLA_HEADS):
        base = h * Q_HEAD_PAD
        rope = base + QK_NOPE
        qt_ref[base:rope, :] = (qall[base:rope, :] * scale).astype(BF16)
        rot = qall[rope:rope + QK_ROPE, :] * cost + qall[rope + QK_ROPE:base + Q_HEAD_PAD, :] * sint
        qt_ref[rope:rope + QK_ROPE, :] = (rot * scale).astype(BF16)
        qt_ref[rope + QK_ROPE:base + Q_HEAD_PAD, :] = jnp.zeros((QK_ROPE, rot.shape[1]), BF16)
    ckv = _rms(ckv_ref[...], kvg_ref[...]).astype(BF16)
    kn_ref[...] = _dot(ckv, wk_ref[...]).astype(BF16)
    vt_ref[...] = _dot_nt(wvt_ref[...], ckv).astype(BF16)
    krot_ref[...] = _rope_pair(kr_ref[...], cosp, sinp).astype(BF16)


def _mla_proj(u, cosp, sinp, cost, sint, qg, kvg, wqt, wk, wvt, seq, tm=512):
    t = u.shape[0]
    nseq = seq // tm
    scale = float((QK_NOPE + QK_ROPE) ** -0.5 * np.log2(np.e))
    row = lambda i: (i, 0)
    const = lambda i: (0, 0)
    return pl.pallas_call(
        functools.partial(_mla_proj_kernel, scale=scale),
        grid=(t // tm,),
        in_specs=[
            pl.BlockSpec((tm, Q_LORA), lambda i: (i, U_CQ // Q_LORA)),
            pl.BlockSpec((tm, KV_LORA), lambda i: (i, U_CKV // KV_LORA)),
            pl.BlockSpec((tm, LANE), lambda i: (i, U_KR // LANE)),
            pl.BlockSpec((tm, LANE), lambda i: (i % nseq, 0)),
            pl.BlockSpec((tm, LANE), lambda i: (i % nseq, 0)),
            pl.BlockSpec((QK_ROPE, tm), lambda i: (0, i % nseq)),
            pl.BlockSpec((QK_ROPE, tm), lambda i: (0, i % nseq)),
            pl.BlockSpec((1, Q_LORA), const),
            pl.BlockSpec((1, KV_LORA), const),
            pl.BlockSpec((MLA_HEADS * Q_HEAD_PAD, Q_LORA), const),
            pl.BlockSpec((KV_LORA, MLA_WIDTH), const),
            pl.BlockSpec((MLA_WIDTH, KV_LORA), const),
        ],
        out_specs=[
            pl.BlockSpec((MLA_HEADS * Q_HEAD_PAD, tm), lambda i: (0, i)),
            pl.BlockSpec((tm, MLA_WIDTH), row),
            pl.BlockSpec((tm, LANE), row),
            pl.BlockSpec((MLA_WIDTH, tm), lambda i: (0, i)),
        ],
        out_shape=[
            jax.ShapeDtypeStruct((MLA_HEADS * Q_HEAD_PAD, t), BF16),
            jax.ShapeDtypeStruct((t, MLA_WIDTH), BF16),
            jax.ShapeDtypeStruct((t, LANE), BF16),
            jax.ShapeDtypeStruct((MLA_WIDTH, t), BF16),
        ],
        compiler_params=pltpu.CompilerParams(
            dimension_semantics=("parallel",), vmem_limit_bytes=VMEM_LIMIT),
        name="mla_proj",
    )(u, u, u, cosp, sinp, cost, sint, qg, kvg, wqt, wk, wvt)


ONES_ROWS = 16


def _attn_kernel(qt_ref, kn_ref, kr_ref, vt_ref, o_ref, s_ref, mrun_ref, acc_ref, *, tq):
    qi = pl.program_id(1)
    sublanes = mrun_ref.shape[1]
    krow = lax.broadcasted_iota(jnp.int32, (tq, tq), 0) // CHUNK
    qcol = lax.broadcasted_iota(jnp.int32, (tq, tq), 1) // CHUNK
    diag_mask = krow <= qcol
    ones = jnp.ones((ONES_ROWS, tq), BF16)

    mrun_ref[...] = jnp.full(mrun_ref.shape, -1e30, F32)
    acc_ref[...] = jnp.zeros_like(acc_ref)

    def scores(j, masked):
        ks = pl.multiple_of(j * tq, tq)
        kr = kr_ref[pl.ds(ks, tq), :]
        for h in range(MLA_HEADS):
            qt = qt_ref[h * Q_HEAD_PAD:(h + 1) * Q_HEAD_PAD, :]
            k = jnp.concatenate([kn_ref[pl.ds(ks, tq), h * QK_NOPE:(h + 1) * QK_NOPE], kr], axis=-1)
            s = _dot(k, qt)
            if masked:
                s = jnp.where(diag_mask, s, -1e30)
            s_ref[h, j] = s
            tile_max = jnp.max(s.reshape(tq // sublanes, sublanes, tq), axis=0)
            mrun_ref[h] = jnp.maximum(mrun_ref[h], tile_max)

    def scores_pair(p, carry):
        scores(2 * p, masked=False)
        scores(2 * p + 1, masked=False)
        return carry

    lax.fori_loop(0, lax.shift_right_logical(qi, 1), scores_pair, 0)

    @pl.when(qi % 2 == 1)
    def _():
        scores(qi - 1, masked=False)

    scores(qi, masked=True)

    m = [jnp.max(mrun_ref[h], axis=0, keepdims=True) for h in range(MLA_HEADS)]

    def accumulate(j):
        ks = pl.multiple_of(j * tq, tq)
        for h in range(MLA_HEADS):
            p = jnp.exp2(s_ref[h, j] - m[h]).astype(BF16)
            vt = jnp.concatenate([vt_ref[h * V_HEAD:(h + 1) * V_HEAD, pl.ds(ks, tq)], ones], axis=0)
            acc_ref[h] += _dot(vt, p)

    def accumulate_pair(p, carry):
        accumulate(2 * p)
        accumulate(2 * p + 1)
        return carry

    lax.fori_loop(0, lax.shift_right_logical(qi + 1, 1), accumulate_pair, 0)

    @pl.when(qi % 2 == 0)
    def _():
        accumulate(qi)

    for h in range(MLA_HEADS):
        acc = acc_ref[h]
        out_t = acc[:V_HEAD, :] / acc[V_HEAD:V_HEAD + 1, :]
        o_ref[:, h * V_HEAD:(h + 1) * V_HEAD] = out_t.T.astype(BF16)


def _mla_attn(qt, kn, kr, vt, batch, seq, tq=256):
    t = kn.shape[0]
    nq = seq // tq
    return pl.pallas_call(
        functools.partial(_attn_kernel, tq=tq),
        grid=(batch, nq),
        in_specs=[
            pl.BlockSpec((MLA_HEADS * Q_HEAD_PAD, tq), lambda b, i: (0, b * nq + i)),
            pl.BlockSpec((seq, MLA_WIDTH), lambda b, i: (b, 0)),
            pl.BlockSpec((seq, LANE), lambda b, i: (b, 0)),
            pl.BlockSpec((MLA_WIDTH, seq), lambda b, i: (0, b)),
        ],
        out_specs=pl.BlockSpec((tq, MLA_WIDTH), lambda b, i: (b * nq + i, 0)),
        out_shape=jax.ShapeDtypeStruct((t, MLA_WIDTH), BF16),
        scratch_shapes=[
            pltpu.VMEM((MLA_HEADS, nq, tq, tq), F32),
            pltpu.VMEM((MLA_HEADS, 8, tq), F32),
            pltpu.VMEM((MLA_HEADS, V_HEAD + ONES_ROWS, tq), F32),
        ],
        compiler_params=pltpu.CompilerParams(
            dimension_semantics=("parallel", "arbitrary"), vmem_limit_bytes=VMEM_LIMIT),
        name="mla_attn",
    )(qt, kn, kr, vt)


def _ssd_kernel(z_ref, xbc_ref, prev_ref, dt_ref, cw_ref, cb_ref, dtb_ref, alog_ref, dexp_ref, ng_ref,
                o_ref, st_ref, xc_ref, dte_ref, acse_ref, wb_ref, *, ts):
    s_idx = pl.program_id(1)
    halo = prev_ref.shape[0]

    @pl.when(s_idx == 0)
    def _():
        st_ref[...] = jnp.zeros_like(st_ref)

    sub = lax.broadcasted_iota(jnp.int32, (halo, SSM_CONV_CH), 0)
    for k in range(SSM_CONV):
        wb_ref[k] = jnp.broadcast_to(cw_ref[k:k + 1, :], (halo, SSM_CONV_CH))
    wb_ref[SSM_CONV] = jnp.broadcast_to(cb_ref[...], (halo, SSM_CONV_CH))

    def conv_tile(i, prev):
        r = pl.multiple_of(i * halo, halo)
        cur = xbc_ref[pl.ds(r, halo), :]
        conv = wb_ref[SSM_CONV] + wb_ref[SSM_CONV - 1] * cur
        for d in range(1, SSM_CONV):
            shifted = jnp.where(sub < d, pltpu.roll(prev, d, axis=0), pltpu.roll(cur, d, axis=0))
            conv = conv + wb_ref[SSM_CONV - 1 - d] * shifted
        xc_ref[pl.ds(r, halo), :] = conv * _sigmoid(conv)
        return cur

    lax.fori_loop(0, ts // halo, conv_tile, jnp.where(s_idx > 0, prev_ref[...], 0.0), unroll=True)

    raw = dt_ref[...] + dtb_ref[...]
    dt = jnp.maximum(raw, 0.0) + jnp.log1p(jnp.exp(-jnp.abs(raw)))
    a = dt * (-jnp.exp(alog_ref[...]))
    lane = lax.broadcasted_iota(jnp.int32, (ts, LANE), 1)
    ri = lax.broadcasted_iota(jnp.int32, (ts, ts), 0)
    ci = lax.broadcasted_iota(jnp.int32, (ts, ts), 1)
    tri = jnp.where((ri // CHUNK == ci // CHUNK) & (ci <= ri), 1.0, 0.0).astype(BF16)
    c3 = _dot(tri, _split3(a, lane))
    acs = c3 + pltpu.roll(c3, LANE - SSM_HEADS, axis=1) + pltpu.roll(c3, LANE - 2 * SSM_HEADS, axis=1)
    er = lax.broadcasted_iota(jnp.int32, (LANE, SSM_INNER), 0)
    ec = lax.broadcasted_iota(jnp.int32, (LANE, SSM_INNER), 1) // SSM_HEAD_DIM
    expand = jnp.where((er % SSM_HEADS == ec) & (er < 3 * SSM_HEADS), 1.0, 0.0).astype(BF16)
    log2e = float(np.log2(np.e))
    both = _dot(jnp.concatenate([_split3(dt, lane), _split3(acs * log2e, lane)], axis=0), expand)
    dte_ref[...] = both[:ts]
    acse_ref[...] = both[ts:]

    li = lax.broadcasted_iota(jnp.int32, (CHUNK, SSM_INNER), 0)
    si = lax.broadcasted_iota(jnp.int32, (CHUNK, SSM_INNER), 1) % SSM_HEAD_DIM
    quad = 4 * SSM_HEAD_DIM
    bd_r = lax.broadcasted_iota(jnp.int32, (quad, quad), 0) // SSM_HEAD_DIM
    bd_c = lax.broadcasted_iota(jnp.int32, (quad, quad), 1) // SSM_HEAD_DIM
    bd_mask = bd_r == bd_c
    dexp = dexp_ref[...]
    ng = ng_ref[...]

    def chunk_body(c, carry):
        r0 = pl.multiple_of(c * CHUNK, CHUNK)
        rows = pl.ds(r0, CHUNK)
        xs = xc_ref[rows, 0:SSM_INNER]
        acx = acse_ref[rows, :]
        last = acx[CHUNK - 1:CHUNK, :]
        xdt = xs * dte_ref[rows, :]
        rv = jnp.sum(jnp.where(li == si, acx, 0.0), axis=0, keepdims=True)
        lmat = jnp.exp2(jnp.where(li >= si, acx - rv, -jnp.inf))
        xdec = (xdt * jnp.exp2(last - acx)).astype(BF16)
        xdt_b = xdt.astype(BF16)
        eacx = jnp.exp2(acx)
        elast = jnp.exp2(last)
        ys = []
        for g in range(SSM_GROUPS):
            gl = slice(g * GROUP_WIDTH, (g + 1) * GROUP_WIDTH)
            b_g = xc_ref[rows, SSM_INNER + g * SSM_STATE:SSM_INNER + (g + 1) * SSM_STATE].astype(BF16)
            c0 = SSM_INNER + SSM_GROUPS * SSM_STATE + g * SSM_STATE
            c_g = xc_ref[rows, c0:c0 + SSM_STATE].astype(BF16)
            gt = _dot_nt(c_g, jnp.concatenate([b_g] * 4, axis=0))
            st_g = st_ref[:, gl]
            y_off = _dot(c_g, st_g.astype(BF16)) * eacx[:, gl]
            yd = []
            for qq in range(GROUP_WIDTH // quad):
                sl = slice(g * GROUP_WIDTH + qq * quad, g * GROUP_WIDTH + (qq + 1) * quad)
                m_q = (gt * lmat[:, sl]).astype(BF16)
                x_q = jnp.concatenate([xdt_b[:, sl]] * 4, axis=0)
                x_q = jnp.where(bd_mask, x_q, jnp.zeros_like(x_q))
                yd.append(_dot(m_q, x_q))
            ys.append(jnp.concatenate(yd, axis=-1) + y_off)
            st_ref[:, gl] = st_g * elast[:, gl] + _dot_tn(b_g, xdec[:, gl])
        y = jnp.concatenate(ys, axis=-1) + xs * dexp
        z = z_ref[rows, :]
        y = y * (z * _sigmoid(z))
        outs = []
        for g in range(SSM_GROUPS):
            yg = y[:, g * GROUP_WIDTH:(g + 1) * GROUP_WIDTH]
            outs.append(yg * lax.rsqrt(jnp.mean(yg * yg, axis=-1, keepdims=True) + EPS))
        o_ref[rows, :] = (jnp.concatenate(outs, axis=-1) * ng).astype(BF16)
        return carry

    lax.fori_loop(0, ts // CHUNK, chunk_body, 0, unroll=True)


def _ssd(u, cw, cb, dtb, alog, dexp, ng, batch, seq, ts=512, halo=8):
    t = u.shape[0]
    ns = seq // ts
    const = lambda b, s: (0, 0)
    rowblk = lambda b, s: b * ns + s
    return pl.pallas_call(
        functools.partial(_ssd_kernel, ts=ts),
        grid=(batch, ns),
        in_specs=[
            pl.BlockSpec((ts, SSM_INNER), lambda b, s: (rowblk(b, s), U_Z // SSM_INNER)),
            pl.BlockSpec((ts, SSM_CONV_CH), lambda b, s: (rowblk(b, s), U_XBC // SSM_CONV_CH)),
            pl.BlockSpec((halo, SSM_CONV_CH),
                         lambda b, s: (jnp.maximum(rowblk(b, s) * (ts // halo) - 1, 0), U_XBC // SSM_CONV_CH)),
            pl.BlockSpec((ts, LANE), lambda b, s: (rowblk(b, s), U_DT // LANE)),
            pl.BlockSpec((SSM_CONV, SSM_CONV_CH), const),
            pl.BlockSpec((1, SSM_CONV_CH), const),
            pl.BlockSpec((1, LANE), const),
            pl.BlockSpec((1, LANE), const),
            pl.BlockSpec((1, SSM_INNER), const),
            pl.BlockSpec((1, SSM_INNER), const),
        ],
        out_specs=pl.BlockSpec((ts, SSM_INNER), lambda b, s: (rowblk(b, s), 0)),
        out_shape=jax.ShapeDtypeStruct((t, SSM_INNER), BF16),
        scratch_shapes=[
            pltpu.VMEM((SSM_STATE, SSM_INNER), F32),
            pltpu.VMEM((ts, SSM_CONV_CH), F32),
            pltpu.VMEM((ts, SSM_INNER), F32),
            pltpu.VMEM((ts, SSM_INNER), F32),
            pltpu.VMEM((SSM_CONV + 1, halo, SSM_CONV_CH), F32),
        ],
        compiler_params=pltpu.CompilerParams(
            dimension_semantics=("parallel", "arbitrary"), vmem_limit_bytes=VMEM_LIMIT),
        name="ssd",
    )(u, u, u, u, cw, cb, dtb, alog, dexp, ng)


def _out_proj_kernel(a_ref, b_ref, x_ref, wa_ref, wb_ref, g1_ref, g2_ref, x1_ref, h2_ref, *, piece):
    for r in range(0, a_ref.shape[0], piece):
        rows = slice(r, r + piece)
        mix = _dot(a_ref[rows, :], wa_ref[...]) + _dot(b_ref[rows, :], wb_ref[...])
        x1 = x_ref[rows, :] + _rms(mix, g1_ref[...])
        x1_ref[rows, :] = x1
        h2_ref[rows, :] = _rms(x1, g2_ref[...]).astype(BF16)


def _out_proj(a_out, b_out, x2, wa, wb, g1, g2, tm=512, piece=256):
    t = x2.shape[0]
    row = lambda i: (i, 0)
    const = lambda i: (0, 0)
    return pl.pallas_call(
        functools.partial(_out_proj_kernel, piece=piece),
        grid=(t // tm,),
        in_specs=[
            pl.BlockSpec((tm, MLA_WIDTH), row),
            pl.BlockSpec((tm, SSM_INNER), row),
            pl.BlockSpec((tm, D_MODEL), row),
            pl.BlockSpec((MLA_WIDTH, D_MODEL), const),
            pl.BlockSpec((SSM_INNER, D_MODEL), const),
            pl.BlockSpec((1, D_MODEL), const),
            pl.BlockSpec((1, D_MODEL), const),
        ],
        out_specs=[pl.BlockSpec((tm, D_MODEL), row), pl.BlockSpec((tm, D_MODEL), row)],
        out_shape=[jax.ShapeDtypeStruct((t, D_MODEL), F32), jax.ShapeDtypeStruct((t, D_MODEL), BF16)],
        compiler_params=pltpu.CompilerParams(
            dimension_semantics=("parallel",), vmem_limit_bytes=VMEM_LIMIT),
        name="out_proj",
    )(a_out, b_out, x2, wa, wb, g1, g2)


def _ffn_kernel(h_ref, x1_hbm, wg_ref, wu_ref, wd_ref, cw_ref, g_ref, o_ref,
                act_ref, tail_ref, x1_ref, x1_sem, *, tm, tf, seq):
    i = pl.program_id(0)
    j = pl.program_id(1)
    nt = pl.num_programs(0) - 1
    nf = tail_ref.shape[0]
    sub = tail_ref.shape[1]
    first = (i == 0) & (j == 0)
    cols = pl.ds(pl.multiple_of(j * tf, tf), tf)

    def x1_copy(tile):
        rows = pl.ds(pl.multiple_of(tile * tm, tm), tm)
        return pltpu.make_async_copy(x1_hbm.at[rows, :], x1_ref, x1_sem.at[0])

    @pl.when((i < nt) & (j == nf - 1))
    def _():
        x1_copy(i).start()

    def produce(slot):
        cw = cw_ref[:, cols]
        prev = tail_ref[j]
        row = lax.broadcasted_iota(jnp.int32, prev.shape, 0)
        c0 = float(np.sqrt(2.0 / np.pi))
        for r in range(0, tm, FFN_ROWS):
            rows = slice(r, r + FFN_ROWS)
            gate = _dot(h_ref[rows, :], wg_ref[...])
            up = _dot(h_ref[rows, :], wu_ref[...])
            pos = (i * tm + r) % seq + lax.broadcasted_iota(jnp.int32, (FFN_ROWS, 1), 0)
            conv = cw[FFN_CONV:FFN_CONV + 1, :] + cw[FFN_CONV - 1:FFN_CONV, :] * gate
            for k in range(FFN_CONV - 1):
                d = FFN_CONV - 1 - k
                rolled = pltpu.roll(gate, d, axis=0)
                head = jnp.where(row < d, pltpu.roll(prev, d, axis=0), rolled[:sub, :])
                shifted = jnp.concatenate([head, rolled[sub:, :]], axis=0)
                conv = conv + cw[k:k + 1, :] * jnp.where(pos >= d, shifted, 0.0)
            act = 0.5 * conv * (1.0 + jnp.tanh(c0 * (conv + 0.044715 * (conv * conv * conv))))
            act_ref[slot, rows, :] = (act * up).astype(BF16)
            prev = gate[FFN_ROWS - sub:, :]
        tail_ref[j] = prev

    def consume(slot):
        o_ref[...] += _dot(act_ref[slot], wd_ref[...])

    @pl.when(first)
    def _():
        tail_ref[...] = jnp.zeros_like(tail_ref)
        produce(0)

    @pl.when((i < nt) & (j == 1))
    def _():
        o_ref[...] = jnp.zeros_like(o_ref)

    for parity in range(2):
        @pl.when(jnp.logical_not(first) & (i < nt) & ((i + j) % 2 == parity))
        def _(parity=parity):
            produce(parity)
            consume(1 - parity)

    @pl.when((i == nt) & (j == 0))
    def _():
        consume((nt * nf - 1) % 2)

    @pl.when(jnp.logical_not(first) & (j == 0))
    def _():
        x1_copy(i - 1).wait()
        o_ref[...] = x1_ref[...] + _rms(o_ref[...], g_ref[...])


def _ffn(h2, x1, wg, wu, wd, cwb, g, seq, tm=1024, tf=512):
    t = h2.shape[0]
    nt = t // tm
    nf = D_FF // tf
    assert nf % 2 == 1
    row_p = lambda i, j: (jnp.minimum(i, nt - 1), 0)
    col_p = lambda i, j: (0, jnp.where(i < nt, j, nf - 1))
    row_c = lambda i, j: (jnp.where(j > 0, jnp.minimum(i, nt - 1), jnp.maximum(i - 1, 0)), 0)
    return pl.pallas_call(
        functools.partial(_ffn_kernel, tm=tm, tf=tf, seq=seq),
        grid=(nt + 1, nf),
        in_specs=[
            pl.BlockSpec((tm, D_MODEL), row_p),
            pl.BlockSpec(memory_space=pl.ANY),
            pl.BlockSpec((D_MODEL, tf), col_p),
            pl.BlockSpec((D_MODEL, tf), col_p),
            pl.BlockSpec((tf, D_MODEL), lambda i, j: (jnp.where(j > 0, j - 1, nf - 1), 0)),
            pl.BlockSpec((FFN_CONV + 1, D_FF), lambda i, j: (0, 0)),
            pl.BlockSpec((1, D_MODEL), lambda i, j: (0, 0)),
        ],
        out_specs=pl.BlockSpec((tm, D_MODEL), row_c),
        out_shape=jax.ShapeDtypeStruct((t, D_MODEL), F32),
        scratch_shapes=[
            pltpu.VMEM((2, tm, tf), BF16),
            pltpu.VMEM((nf, 8, tf), F32),
            pltpu.VMEM((tm, D_MODEL), F32),
            pltpu.SemaphoreType.DMA((1,)),
        ],
        compiler_params=pltpu.CompilerParams(
            dimension_semantics=("arbitrary", "arbitrary"), vmem_limit_bytes=VMEM_LIMIT),
        name="ffn",
    )(h2, x1, wg, wu, wd, cwb, g)


def _swap_half(w):
    half = w.shape[-1] // 2
    return jnp.concatenate([-w[..., half:], w[..., :half]], axis=-1)


def _rope_tables(seq):
    inv = 1.0 / (ROPE_THETA ** (jnp.arange(0, QK_ROPE, 2, dtype=F32) / QK_ROPE))
    ang = jnp.arange(seq, dtype=F32)[:, None] * inv[None, :]
    zeros = jnp.zeros((seq, LANE - QK_ROPE), F32)
    cos2 = jnp.concatenate([jnp.cos(ang), jnp.cos(ang)], axis=-1)
    sin2 = jnp.concatenate([jnp.sin(ang), jnp.sin(ang)], axis=-1)
    cosp = jnp.concatenate([cos2, zeros], axis=-1)
    sinp = jnp.concatenate([sin2, zeros], axis=-1)
    return cosp, sinp, cos2.T, sin2.T


def _layer(x2, batch, seq, mix_pre_g, w_in, q_norm_g, w_uq, kv_norm_g, w_ukv, ssm_conv_w, ssm_conv_b,
           dt_bias, a_log, d_skip, ssm_norm_g, w_out, mix_post_g, ffn_pre_g, w_gate, w_up,
           ffn_conv_w, ffn_conv_b, w_down, ffn_post_g):
    row = lambda v: v.reshape(1, -1).astype(F32)
    pad_lane = lambda v: jnp.pad(v.astype(F32), (0, LANE - v.shape[0])).reshape(1, LANE)

    w_in_r = _regroup_w_in(w_in)

    wq3 = w_uq.reshape(Q_LORA, MLA_HEADS, QK_NOPE + QK_ROPE)
    wq_rope = wq3[..., QK_NOPE:]
    wq_r = jnp.concatenate([wq3[..., :QK_NOPE], wq_rope, _swap_half(wq_rope)], axis=-1)
    wqt_r = wq_r.reshape(Q_LORA, MLA_HEADS * Q_HEAD_PAD).T.astype(BF16)
    wkv3 = w_ukv.reshape(KV_LORA, MLA_HEADS, QK_NOPE + V_HEAD)
    wk_r = wkv3[..., :QK_NOPE].reshape(KV_LORA, MLA_WIDTH).astype(BF16)
    wvt_r = wkv3[..., QK_NOPE:].reshape(KV_LORA, MLA_WIDTH).T.astype(BF16)

    u = _in_proj(x2, row(mix_pre_g), w_in_r)
    cosp, sinp, cost, sint = _rope_tables(seq)
    qt, kn, kr, vt = _mla_proj(u, cosp, sinp, cost, sint, row(q_norm_g), row(kv_norm_g), wqt_r, wk_r, wvt_r, seq)
    a_out = _mla_attn(qt, kn, kr, vt, batch, seq)
    b_out = _ssd(u, ssm_conv_w.astype(F32), row(ssm_conv_b), pad_lane(dt_bias), pad_lane(a_log),
                 row(jnp.repeat(d_skip, SSM_HEAD_DIM)), row(ssm_norm_g), batch, seq)
    w_out_b = w_out.astype(BF16)
    x1, h2 = _out_proj(a_out, b_out, x2, w_out_b[:MLA_WIDTH], w_out_b[MLA_WIDTH:],
                       row(mix_post_g), row(ffn_pre_g))
    return _ffn(h2, x1, w_gate.astype(BF16), w_up.astype(BF16), w_down.astype(BF16),
                jnp.concatenate([ffn_conv_w.astype(F32), row(ffn_conv_b)], axis=0), row(ffn_post_g), seq)


def kernel(x, mix_pre_g, w_in, q_norm_g, w_uq, kv_norm_g, w_ukv, ssm_conv_w, ssm_conv_b, dt_bias, a_log,
           d_skip, ssm_norm_g, w_out, mix_post_g, ffn_pre_g, w_gate, w_up, ffn_conv_w, ffn_conv_b,
           w_down, ffn_post_g):
    batch, seq, _ = x.shape
    x2 = x.reshape(batch * seq, D_MODEL)
    for l in range(mix_pre_g.shape[0]):
        x2 = _layer(x2, batch, seq, mix_pre_g[l], w_in[l], q_norm_g[l], w_uq[l], kv_norm_g[l], w_ukv[l],
                    ssm_conv_w[l], ssm_conv_b[l], dt_bias[l], a_log[l], d_skip[l], ssm_norm_g[l],
                    w_out[l], mix_post_g[l], ffn_pre_g[l], w_gate[l], w_up[l], ffn_conv_w[l],
                    ffn_conv_b[l], w_down[l], ffn_post_g[l])
    return x2.reshape(batch, seq, D_MODEL)
```

```python
import functools

import jax
import jax.numpy as jnp
import numpy as np
from jax import lax
from jax.experimental import pallas as pl
from jax.experimental.pallas import tpu as pltpu

F32 = jnp.float32
BF16 = jnp.bfloat16

D_MODEL = 2048
CHUNK = 64
EPS = 1e-6

MLA_HEADS = 8
Q_LORA = 768
KV_LORA = 512
QK_NOPE = 128
QK_ROPE = 64
V_HEAD = 128
ROPE_THETA = 10000.0
MLA_WIDTH = MLA_HEADS * V_HEAD
Q_HEAD_PAD = 256

SSM_HEADS = 16
SSM_HEAD_DIM = 64
SSM_INNER = SSM_HEADS * SSM_HEAD_DIM
SSM_GROUPS = 2
SSM_STATE = 128
SSM_CONV = 4
SSM_CONV_CH = SSM_INNER + 2 * SSM_GROUPS * SSM_STATE
GROUP_WIDTH = SSM_INNER // SSM_GROUPS

D_FF = 5632
FFN_CONV = 3
FFN_ROWS = 256

LANE = 128
U_COLS = 4096
U_CQ = 0
U_KR = Q_LORA
U_DT = U_KR + LANE
U_CKV = U_DT + LANE
U_XBC = U_CKV + KV_LORA
U_Z = U_XBC + SSM_CONV_CH

VMEM_LIMIT = 56 * 1024 * 1024


def _rms(x, g):
    return x * lax.rsqrt(jnp.mean(x * x, axis=-1, keepdims=True) + EPS) * g


def _sigmoid(x):
    return 1.0 / (1.0 + jnp.exp(-x))


def _dot(a, b):
    return jnp.dot(a, b, preferred_element_type=F32)


def _dot_nt(a, b):
    return lax.dot_general(a, b, (((1,), (1,)), ((), ())), preferred_element_type=F32)


def _dot_tn(a, b):
    return lax.dot_general(a, b, (((0,), (0,)), ((), ())), preferred_element_type=F32)


def _split3(x, lane):
    xm = jnp.where(lane < SSM_HEADS, x, 0.0)
    hi = xm.astype(BF16).astype(F32)
    r1 = xm - hi
    mid = r1.astype(BF16).astype(F32)
    lo = (r1 - mid).astype(BF16).astype(F32)
    packed = hi + pltpu.roll(mid, SSM_HEADS, axis=1) + pltpu.roll(lo, 2 * SSM_HEADS, axis=1)
    return packed.astype(BF16)


def _in_proj_kernel(x_ref, g_ref, w_ref, o_ref):
    x = x_ref[...]
    r = lax.rsqrt(jnp.mean(x * x, axis=-1, keepdims=True) + EPS)
    o_ref[...] = _dot((x * g_ref[...]).astype(BF16), w_ref[...]) * r


def _in_proj(x2, g, w, tm=512):
    t = x2.shape[0]
    return pl.pallas_call(
        _in_proj_kernel,
        grid=(t // tm,),
        in_specs=[
            pl.BlockSpec((tm, D_MODEL), lambda i: (i, 0)),
            pl.BlockSpec((1, D_MODEL), lambda i: (0, 0)),
            pl.BlockSpec((D_MODEL, U_COLS), lambda i: (0, 0), pipeline_mode=pl.Buffered(1)),
        ],
        out_specs=pl.BlockSpec((tm, U_COLS), lambda i: (i, 0)),
        out_shape=jax.ShapeDtypeStruct((t, U_COLS), F32),
        compiler_params=pltpu.CompilerParams(
            dimension_semantics=("parallel",), vmem_limit_bytes=VMEM_LIMIT),
        name="in_proj",
    )(x2, g, w)


def _regroup_kernel(w_ref, o_ref):
    src = np.cumsum([0, Q_LORA, KV_LORA, QK_ROPE, SSM_INNER, SSM_CONV_CH]).tolist()
    s_cq, s_ckv, s_kr, s_z, s_xbc, s_dt = src
    for dst, lo, width in ((U_CQ, s_cq, Q_LORA), (U_CKV, s_ckv, KV_LORA), (U_XBC, s_xbc, SSM_CONV_CH),
                           (U_Z, s_z, SSM_INNER), (U_KR, s_kr, QK_ROPE), (U_DT, s_dt, SSM_HEADS)):
        o_ref[:, dst:dst + width] = w_ref[:, lo:lo + width].astype(BF16)
    o_ref[:, U_KR + QK_ROPE:U_DT] = _swap_half(w_ref[:, s_kr:s_kr + QK_ROPE]).astype(BF16)
    o_ref[:, U_DT + SSM_HEADS:U_CKV] = jnp.zeros((o_ref.shape[0], LANE - SSM_HEADS), BF16)


def _regroup_w_in(w_in, tr=256):
    return pl.pallas_call(
        _regroup_kernel,
        grid=(D_MODEL // tr,),
        in_specs=[pl.BlockSpec((tr, w_in.shape[1]), lambda i: (i, 0))],
        out_specs=pl.BlockSpec((tr, U_COLS), lambda i: (i, 0)),
        out_shape=jax.ShapeDtypeStruct((D_MODEL, U_COLS), BF16),
        compiler_params=pltpu.CompilerParams(
            dimension_semantics=("parallel",), vmem_limit_bytes=VMEM_LIMIT),
        name="regroup_w_in",
    )(w_in)


def _rope_pair(blk, cosp, sinp):
    return blk * cosp + pltpu.roll(blk, QK_ROPE, axis=1) * sinp


def _mla_proj_kernel(cq_ref, ckv_ref, kr_ref, cos_ref, sin_ref, cost_ref, sint_ref, qg_ref, kvg_ref,
                     wqt_ref, wk_ref, wvt_ref, qt_ref, kn_ref, krot_ref, vt_ref, *, scale):
    cosp = cos_ref[...]
    sinp = sin_ref[...]
    cost = cost_ref[...]
    sint = sint_ref[...]
    qall = _dot_nt(wqt_ref[...], _rms(cq_ref[...], qg_ref[...]).astype(BF16))
    for h in range(MLA_HEADS):
        base = h * Q_HEAD_PAD
        rope = base + QK_NOPE
        qt_ref[base:rope, :] = (qall[base:rope, :] * scale).astype(BF16)
        rot = qall[rope:rope + QK_ROPE, :] * cost + qall[rope + QK_ROPE:base + Q_HEAD_PAD, :] * sint
        qt_ref[rope:rope + QK_ROPE, :] = (rot * scale).astype(BF16)
        qt_ref[rope + QK_ROPE:base + Q_HEAD_PAD, :] = jnp.zeros((QK_ROPE, rot.shape[1]), BF16)
    ckv = _rms(ckv_ref[...], kvg_ref[...]).astype(BF16)
    kn_ref[...] = _dot(ckv, wk_ref[...]).astype(BF16)
    vt_ref[...] = _dot_nt(wvt_ref[...], ckv).astype(BF16)
    krot_ref[...] = _rope_pair(kr_ref[...], cosp, sinp).astype(BF16)


def _mla_proj(u, cosp, sinp, cost, sint, qg, kvg, wqt, wk, wvt, seq, tm=1024):
    t = u.shape[0]
    nseq = seq // tm
    scale = float((QK_NOPE + QK_ROPE) ** -0.5 * np.log2(np.e))
    row = lambda i: (i, 0)
    const = lambda i: (0, 0)
    return pl.pallas_call(
        functools.partial(_mla_proj_kernel, scale=scale),
        grid=(t // tm,),
        in_specs=[
            pl.BlockSpec((tm, Q_LORA), lambda i: (i, U_CQ // Q_LORA)),
            pl.BlockSpec((tm, KV_LORA), lambda i: (i, U_CKV // KV_LORA)),
            pl.BlockSpec((tm, LANE), lambda i: (i, U_KR // LANE)),
            pl.BlockSpec((tm, LANE), lambda i: (i % nseq, 0)),
            pl.BlockSpec((tm, LANE), lambda i: (i % nseq, 0)),
            pl.BlockSpec((QK_ROPE, tm), lambda i: (0, i % nseq)),
            pl.BlockSpec((QK_ROPE, tm), lambda i: (0, i % nseq)),
            pl.BlockSpec((1, Q_LORA), const),
            pl.BlockSpec((1, KV_LORA), const),
            pl.BlockSpec((MLA_HEADS * Q_HEAD_PAD, Q_LORA), const),
            pl.BlockSpec((KV_LORA, MLA_WIDTH), const),
            pl.BlockSpec((MLA_WIDTH, KV_LORA), const),
        ],
        out_specs=[
            pl.BlockSpec((MLA_HEADS * Q_HEAD_PAD, tm), lambda i: (0, i)),
            pl.BlockSpec((tm, MLA_WIDTH), row),
            pl.BlockSpec((tm, LANE), row),
            pl.BlockSpec((MLA_WIDTH, tm), lambda i: (0, i)),
        ],
        out_shape=[
            jax.ShapeDtypeStruct((MLA_HEADS * Q_HEAD_PAD, t), BF16),
            jax.ShapeDtypeStruct((t, MLA_WIDTH), BF16),
            jax.ShapeDtypeStruct((t, LANE), BF16),
            jax.ShapeDtypeStruct((MLA_WIDTH, t), BF16),
        ],
        compiler_params=pltpu.CompilerParams(
            dimension_semantics=("parallel",), vmem_limit_bytes=VMEM_LIMIT),
        name="mla_proj",
    )(u, u, u, cosp, sinp, cost, sint, qg, kvg, wqt, wk, wvt)


ONES_ROWS = 16


def _attn_kernel(qt_ref, kn_ref, kr_ref, vt_ref, o_ref, s_ref, mrun_ref, acc_ref, *, tq):
    qi = pl.program_id(1)
    sublanes = mrun_ref.shape[1]
    krow = lax.broadcasted_iota(jnp.int32, (tq, tq), 0) // CHUNK
    qcol = lax.broadcasted_iota(jnp.int32, (tq, tq), 1) // CHUNK
    diag_mask = krow <= qcol
    ones = jnp.ones((ONES_ROWS, tq), BF16)

    mrun_ref[...] = jnp.full(mrun_ref.shape, -1e30, F32)
    acc_ref[...] = jnp.zeros_like(acc_ref)

    def scores(j, masked):
        ks = pl.multiple_of(j * tq, tq)
        kr = kr_ref[pl.ds(ks, tq), :]
        for h in range(MLA_HEADS):
            qt = qt_ref[h * Q_HEAD_PAD:(h + 1) * Q_HEAD_PAD, :]
            k = jnp.concatenate([kn_ref[pl.ds(ks, tq), h * QK_NOPE:(h + 1) * QK_NOPE], kr], axis=-1)
            s = _dot(k, qt)
            if masked:
                s = jnp.where(diag_mask, s, -1e30)
            s_ref[h, j] = s
            tile_max = jnp.max(s.reshape(tq // sublanes, sublanes, tq), axis=0)
            mrun_ref[h] = jnp.maximum(mrun_ref[h], tile_max)

    def scores_pair(p, carry):
        scores(2 * p, masked=False)
        scores(2 * p + 1, masked=False)
        return carry

    lax.fori_loop(0, lax.shift_right_logical(qi, 1), scores_pair, 0)

    @pl.when(qi % 2 == 1)
    def _():
        scores(qi - 1, masked=False)

    scores(qi, masked=True)

    m = [jnp.max(mrun_ref[h], axis=0, keepdims=True) for h in range(MLA_HEADS)]

    def accumulate(j):
        ks = pl.multiple_of(j * tq, tq)
        for h in range(MLA_HEADS):
            p = jnp.exp2(s_ref[h, j] - m[h]).astype(BF16)
            vt = jnp.concatenate([vt_ref[h * V_HEAD:(h + 1) * V_HEAD, pl.ds(ks, tq)], ones], axis=0)
            acc_ref[h] += _dot(vt, p)

    def accumulate_pair(p, carry):
        accumulate(2 * p)
        accumulate(2 * p + 1)
        return carry

    lax.fori_loop(0, lax.shift_right_logical(qi + 1, 1), accumulate_pair, 0)

    @pl.when(qi % 2 == 0)
    def _():
        accumulate(qi)

    for h in range(MLA_HEADS):
        acc = acc_ref[h]
        out_t = acc[:V_HEAD, :] / acc[V_HEAD:V_HEAD + 1, :]
        o_ref[:, h * V_HEAD:(h + 1) * V_HEAD] = out_t.T.astype(BF16)


def _mla_attn(qt, kn, kr, vt, batch, seq, tq=256):
    t = kn.shape[0]
    nq = seq // tq
    return pl.pallas_call(
        functools.partial(_attn_kernel, tq=tq),
        grid=(batch, nq),
        in_specs=[
            pl.BlockSpec((MLA_HEADS * Q_HEAD_PAD, tq), lambda b, i: (0, b * nq + i)),
            pl.BlockSpec((seq, MLA_WIDTH), lambda b, i: (b, 0)),
            pl.BlockSpec((seq, LANE), lambda b, i: (b, 0)),
            pl.BlockSpec((MLA_WIDTH, seq), lambda b, i: (0, b)),
        ],
        out_specs=pl.BlockSpec((tq, MLA_WIDTH), lambda b, i: (b * nq + i, 0)),
        out_shape=jax.ShapeDtypeStruct((t, MLA_WIDTH), BF16),
        scratch_shapes=[
            pltpu.VMEM((MLA_HEADS, nq, tq, tq), F32),
            pltpu.VMEM((MLA_HEADS, 8, tq), F32),
            pltpu.VMEM((MLA_HEADS, V_HEAD + ONES_ROWS, tq), F32),
        ],
        compiler_params=pltpu.CompilerParams(
            dimension_semantics=("parallel", "arbitrary"), vmem_limit_bytes=VMEM_LIMIT),
        name="mla_attn",
    )(qt, kn, kr, vt)


def _ssd_kernel(z_ref, xbc_ref, prev_ref, dt_ref, cw_ref, cb_ref, dtb_ref, alog_ref, dexp_ref, ng_ref,
                o_ref, st_ref, xc_ref, dte_ref, acse_ref, wb_ref, *, ts):
    s_idx = pl.program_id(1)
    halo = prev_ref.shape[0]

    @pl.when(s_idx == 0)
    def _():
        st_ref[...] = jnp.zeros_like(st_ref)

    sub = lax.broadcasted_iota(jnp.int32, (halo, SSM_CONV_CH), 0)
    for k in range(SSM_CONV):
        wb_ref[k] = jnp.broadcast_to(cw_ref[k:k + 1, :], (halo, SSM_CONV_CH))
    wb_ref[SSM_CONV] = jnp.broadcast_to(cb_ref[...], (halo, SSM_CONV_CH))

    def conv_tile(i, prev):
        r = pl.multiple_of(i * halo, halo)
        cur = xbc_ref[pl.ds(r, halo), :]
        conv = wb_ref[SSM_CONV] + wb_ref[SSM_CONV - 1] * cur
        for d in range(1, SSM_CONV):
            shifted = jnp.where(sub < d, pltpu.roll(prev, d, axis=0), pltpu.roll(cur, d, axis=0))
            conv = conv + wb_ref[SSM_CONV - 1 - d] * shifted
        xc_ref[pl.ds(r, halo), :] = conv * _sigmoid(conv)
        return cur

    lax.fori_loop(0, ts // halo, conv_tile, jnp.where(s_idx > 0, prev_ref[...], 0.0), unroll=True)

    raw = dt_ref[...] + dtb_ref[...]
    dt = jnp.maximum(raw, 0.0) + jnp.log1p(jnp.exp(-jnp.abs(raw)))
    a = dt * (-jnp.exp(alog_ref[...]))
    lane = lax.broadcasted_iota(jnp.int32, (ts, LANE), 1)
    ri = lax.broadcasted_iota(jnp.int32, (ts, ts), 0)
    ci = lax.broadcasted_iota(jnp.int32, (ts, ts), 1)
    tri = jnp.where((ri // CHUNK == ci // CHUNK) & (ci <= ri), 1.0, 0.0).astype(BF16)
    c3 = _dot(tri, _split3(a, lane))
    acs = c3 + pltpu.roll(c3, LANE - SSM_HEADS, axis=1) + pltpu.roll(c3, LANE - 2 * SSM_HEADS, axis=1)
    er = lax.broadcasted_iota(jnp.int32, (LANE, SSM_INNER), 0)
    ec = lax.broadcasted_iota(jnp.int32, (LANE, SSM_INNER), 1) // SSM_HEAD_DIM
    expand = jnp.where((er % SSM_HEADS == ec) & (er < 3 * SSM_HEADS), 1.0, 0.0).astype(BF16)
    log2e = float(np.log2(np.e))
    both = _dot(jnp.concatenate([_split3(dt, lane), _split3(acs * log2e, lane)], axis=0), expand)
    dte_ref[...] = both[:ts]
    acse_ref[...] = both[ts:]

    li = lax.broadcasted_iota(jnp.int32, (CHUNK, SSM_INNER), 0)
    si = lax.broadcasted_iota(jnp.int32, (CHUNK, SSM_INNER), 1) % SSM_HEAD_DIM
    quad = 4 * SSM_HEAD_DIM
    bd_r = lax.broadcasted_iota(jnp.int32, (quad, quad), 0) // SSM_HEAD_DIM
    bd_c = lax.broadcasted_iota(jnp.int32, (quad, quad), 1) // SSM_HEAD_DIM
    bd_mask = bd_r == bd_c
    dexp = dexp_ref[...]
    ng = ng_ref[...]

    def chunk_body(c, carry):
        r0 = pl.multiple_of(c * CHUNK, CHUNK)
        rows = pl.ds(r0, CHUNK)
        xs = xc_ref[rows, 0:SSM_INNER]
        acx = acse_ref[rows, :]
        last = acx[CHUNK - 1:CHUNK, :]
        xdt = xs * dte_ref[rows, :]
        rv = jnp.sum(jnp.where(li == si, acx, 0.0), axis=0, keepdims=True)
        lmat = jnp.exp2(jnp.where(li >= si, acx - rv, -jnp.inf))
        xdec = (xdt * jnp.exp2(last - acx)).astype(BF16)
        xdt_b = xdt.astype(BF16)
        eacx = jnp.exp2(acx)
        elast = jnp.exp2(last)
        ys = []
        for g in range(SSM_GROUPS):
            gl = slice(g * GROUP_WIDTH, (g + 1) * GROUP_WIDTH)
            b_g = xc_ref[rows, SSM_INNER + g * SSM_STATE:SSM_INNER + (g + 1) * SSM_STATE].astype(BF16)
            c0 = SSM_INNER + SSM_GROUPS * SSM_STATE + g * SSM_STATE
            c_g = xc_ref[rows, c0:c0 + SSM_STATE].astype(BF16)
            gt = _dot_nt(c_g, jnp.concatenate([b_g] * 4, axis=0))
            st_g = st_ref[:, gl]
            y_off = _dot(c_g, st_g.astype(BF16)) * eacx[:, gl]
            yd = []
            for qq in range(GROUP_WIDTH // quad):
                sl = slice(g * GROUP_WIDTH + qq * quad, g * GROUP_WIDTH + (qq + 1) * quad)
                m_q = (gt * lmat[:, sl]).astype(BF16)
                x_q = jnp.concatenate([xdt_b[:, sl]] * 4, axis=0)
                x_q = jnp.where(bd_mask, x_q, jnp.zeros_like(x_q))
                yd.append(_dot(m_q, x_q))
            ys.append(jnp.concatenate(yd, axis=-1) + y_off)
            st_ref[:, gl] = st_g * elast[:, gl] + _dot_tn(b_g, xdec[:, gl])
        y = jnp.concatenate(ys, axis=-1) + xs * dexp
        z = z_ref[rows, :]
        y = y * (z * _sigmoid(z))
        outs = []
        for g in range(SSM_GROUPS):
            yg = y[:, g * GROUP_WIDTH:(g + 1) * GROUP_WIDTH]
            outs.append(yg * lax.rsqrt(jnp.mean(yg * yg, axis=-1, keepdims=True) + EPS))
        o_ref[rows, :] = (jnp.concatenate(outs, axis=-1) * ng).astype(BF16)
        return carry

    lax.fori_loop(0, ts // CHUNK, chunk_body, 0, unroll=True)


def _ssd(u, cw, cb, dtb, alog, dexp, ng, batch, seq, ts=512, halo=8):
    t = u.shape[0]
    ns = seq // ts
    const = lambda b, s: (0, 0)
    rowblk = lambda b, s: b * ns + s
    return pl.pallas_call(
        functools.partial(_ssd_kernel, ts=ts),
        grid=(batch, ns),
        in_specs=[
            pl.BlockSpec((ts, SSM_INNER), lambda b, s: (rowblk(b, s), U_Z // SSM_INNER)),
            pl.BlockSpec((ts, SSM_CONV_CH), lambda b, s: (rowblk(b, s), U_XBC // SSM_CONV_CH)),
            pl.BlockSpec((halo, SSM_CONV_CH),
                         lambda b, s: (jnp.maximum(rowblk(b, s) * (ts // halo) - 1, 0), U_XBC // SSM_CONV_CH)),
            pl.BlockSpec((ts, LANE), lambda b, s: (rowblk(b, s), U_DT // LANE)),
            pl.BlockSpec((SSM_CONV, SSM_CONV_CH), const),
            pl.BlockSpec((1, SSM_CONV_CH), const),
            pl.BlockSpec((1, LANE), const),
            pl.BlockSpec((1, LANE), const),
            pl.BlockSpec((1, SSM_INNER), const),
            pl.BlockSpec((1, SSM_INNER), const),
        ],
        out_specs=pl.BlockSpec((ts, SSM_INNER), lambda b, s: (rowblk(b, s), 0)),
        out_shape=jax.ShapeDtypeStruct((t, SSM_INNER), BF16),
        scratch_shapes=[
            pltpu.VMEM((SSM_STATE, SSM_INNER), F32),
            pltpu.VMEM((ts, SSM_CONV_CH), F32),
            pltpu.VMEM((ts, SSM_INNER), F32),
            pltpu.VMEM((ts, SSM_INNER), F32),
            pltpu.VMEM((SSM_CONV + 1, halo, SSM_CONV_CH), F32),
        ],
        compiler_params=pltpu.CompilerParams(
            dimension_semantics=("parallel", "arbitrary"), vmem_limit_bytes=VMEM_LIMIT),
        name="ssd",
    )(u, u, u, u, cw, cb, dtb, alog, dexp, ng)


def _out_proj_kernel(a_ref, b_ref, x_ref, wa_ref, wb_ref, g1_ref, g2_ref, x1_ref, h2_ref, *, piece):
    for r in range(0, a_ref.shape[0], piece):
        rows = slice(r, r + piece)
        mix = _dot(a_ref[rows, :], wa_ref[...]) + _dot(b_ref[rows, :], wb_ref[...])
        x1 = x_ref[rows, :] + _rms(mix, g1_ref[...])
        x1_ref[rows, :] = x1
        h2_ref[rows, :] = _rms(x1, g2_ref[...]).astype(BF16)


def _out_proj(a_out, b_out, x2, wa, wb, g1, g2, tm=512, piece=128):
    t = x2.shape[0]
    row = lambda i: (i, 0)
    const = lambda i: (0, 0)
    return pl.pallas_call(
        functools.partial(_out_proj_kernel, piece=piece),
        grid=(t // tm,),
        in_specs=[
            pl.BlockSpec((tm, MLA_WIDTH), row),
            pl.BlockSpec((tm, SSM_INNER), row),
            pl.BlockSpec((tm, D_MODEL), row),
            pl.BlockSpec((MLA_WIDTH, D_MODEL), const),
            pl.BlockSpec((SSM_INNER, D_MODEL), const),
            pl.BlockSpec((1, D_MODEL), const),
            pl.BlockSpec((1, D_MODEL), const),
        ],
        out_specs=[pl.BlockSpec((tm, D_MODEL), row), pl.BlockSpec((tm, D_MODEL), row)],
        out_shape=[jax.ShapeDtypeStruct((t, D_MODEL), F32), jax.ShapeDtypeStruct((t, D_MODEL), BF16)],
        compiler_params=pltpu.CompilerParams(
            dimension_semantics=("parallel",), vmem_limit_bytes=VMEM_LIMIT),
        name="out_proj",
    )(a_out, b_out, x2, wa, wb, g1, g2)


def _ffn_kernel(h_ref, x1_hbm, wg_ref, wu_ref, wd_ref, cw_ref, g_ref, o_ref,
                act_ref, tail_ref, x1_ref, x1_sem, *, tm, tf, seq):
    i = pl.program_id(0)
    j = pl.program_id(1)
    nt = pl.num_programs(0) - 1
    nf = tail_ref.shape[0]
    sub = tail_ref.shape[1]
    first = (i == 0) & (j == 0)
    cols = pl.ds(pl.multiple_of(j * tf, tf), tf)

    def x1_copy(tile):
        rows = pl.ds(pl.multiple_of(tile * tm, tm), tm)
        return pltpu.make_async_copy(x1_hbm.at[rows, :], x1_ref, x1_sem.at[0])

    @pl.when((i < nt) & (j == nf - 1))
    def _():
        x1_copy(i).start()

    def produce(slot):
        cw = cw_ref[:, cols]
        prev = tail_ref[j]
        row = lax.broadcasted_iota(jnp.int32, prev.shape, 0)
        c0 = float(np.sqrt(2.0 / np.pi))
        for r in range(0, tm, FFN_ROWS):
            rows = slice(r, r + FFN_ROWS)
            gate = _dot(h_ref[rows, :], wg_ref[...])
            up = _dot(h_ref[rows, :], wu_ref[...])
            pos = (i * tm + r) % seq + lax.broadcasted_iota(jnp.int32, (FFN_ROWS, 1), 0)
            conv = cw[FFN_CONV:FFN_CONV + 1, :] + cw[FFN_CONV - 1:FFN_CONV, :] * gate
            for k in range(FFN_CONV - 1):
                d = FFN_CONV - 1 - k
                rolled = pltpu.roll(gate, d, axis=0)
                head = jnp.where(row < d, pltpu.roll(prev, d, axis=0), rolled[:sub, :])
                shifted = jnp.concatenate([head, rolled[sub:, :]], axis=0)
                conv = conv + cw[k:k + 1, :] * jnp.where(pos >= d, shifted, 0.0)
            act = 0.5 * conv * (1.0 + jnp.tanh(c0 * (conv + 0.044715 * (conv * conv * conv))))
            act_ref[slot, rows, :] = (act * up).astype(BF16)
            prev = gate[FFN_ROWS - sub:, :]
        tail_ref[j] = prev

    def consume(slot):
        o_ref[...] += _dot(act_ref[slot], wd_ref[...])

    @pl.when(first)
    def _():
        tail_ref[...] = jnp.zeros_like(tail_ref)
        produce(0)

    @pl.when((i < nt) & (j == 1))
    def _():
        o_ref[...] = jnp.zeros_like(o_ref)

    def finish():
        o_ref[...] = x1_ref[...] + _rms(o_ref[...], g_ref[...])

    for parity in range(2):
        @pl.when((i < nt) & (j > 0) & ((i + j) % 2 == parity))
        def _(parity=parity):
            produce(parity)
            consume(1 - parity)

        @pl.when((i > 0) & (i < nt) & (j == 0) & (i % 2 == parity))
        def _(parity=parity):
            x1_copy(i - 1).wait()
            produce(parity)
            consume(1 - parity)
            finish()

    @pl.when((i == nt) & (j == 0))
    def _():
        x1_copy(i - 1).wait()
        consume((nt * nf - 1) % 2)
        finish()


def _ffn(h2, x1, wg, wu, wd, cwb, g, seq, tm=1024, tf=512):
    t = h2.shape[0]
    nt = t // tm
    nf = D_FF // tf
    assert nf % 2 == 1
    row_p = lambda i, j: (jnp.minimum(i, nt - 1), 0)
    col_p = lambda i, j: (0, jnp.where(i < nt, j, nf - 1))
    row_c = lambda i, j: (jnp.where(j > 0, jnp.minimum(i, nt - 1), jnp.maximum(i - 1, 0)), 0)
    return pl.pallas_call(
        functools.partial(_ffn_kernel, tm=tm, tf=tf, seq=seq),
        grid=(nt + 1, nf),
        in_specs=[
            pl.BlockSpec((tm, D_MODEL), row_p),
            pl.BlockSpec(memory_space=pl.ANY),
            pl.BlockSpec((D_MODEL, tf), col_p),
            pl.BlockSpec((D_MODEL, tf), col_p),
            pl.BlockSpec((tf, D_MODEL), lambda i, j: (jnp.where(j > 0, j - 1, nf - 1), 0)),
            pl.BlockSpec((FFN_CONV + 1, D_FF), lambda i, j: (0, 0)),
            pl.BlockSpec((1, D_MODEL), lambda i, j: (0, 0)),
        ],
        out_specs=pl.BlockSpec((tm, D_MODEL), row_c),
        out_shape=jax.ShapeDtypeStruct((t, D_MODEL), F32),
        scratch_shapes=[
            pltpu.VMEM((2, tm, tf), BF16),
            pltpu.VMEM((nf, 8, tf), F32),
            pltpu.VMEM((tm, D_MODEL), F32),
            pltpu.SemaphoreType.DMA((1,)),
        ],
        compiler_params=pltpu.CompilerParams(
            dimension_semantics=("arbitrary", "arbitrary"), vmem_limit_bytes=VMEM_LIMIT),
        name="ffn",
    )(h2, x1, wg, wu, wd, cwb, g)


def _swap_half(w):
    half = w.shape[-1] // 2
    return jnp.concatenate([-w[..., half:], w[..., :half]], axis=-1)


def _rope_tables(seq):
    inv = 1.0 / (ROPE_THETA ** (jnp.arange(0, QK_ROPE, 2, dtype=F32) / QK_ROPE))
    ang = jnp.arange(seq, dtype=F32)[:, None] * inv[None, :]
    zeros = jnp.zeros((seq, LANE - QK_ROPE), F32)
    cos2 = jnp.concatenate([jnp.cos(ang), jnp.cos(ang)], axis=-1)
    sin2 = jnp.concatenate([jnp.sin(ang), jnp.sin(ang)], axis=-1)
    cosp = jnp.concatenate([cos2, zeros], axis=-1)
    sinp = jnp.concatenate([sin2, zeros], axis=-1)
    return cosp, sinp, cos2.T, sin2.T


def _layer(x2, batch, seq, mix_pre_g, w_in, q_norm_g, w_uq, kv_norm_g, w_ukv, ssm_conv_w, ssm_conv_b,
           dt_bias, a_log, d_skip, ssm_norm_g, w_out, mix_post_g, ffn_pre_g, w_gate, w_up,
           ffn_conv_w, ffn_conv_b, w_down, ffn_post_g):
    row = lambda v: v.reshape(1, -1).astype(F32)
    pad_lane = lambda v: jnp.pad(v.astype(F32), (0, LANE - v.shape[0])).reshape(1, LANE)

    w_in_r = _regroup_w_in(w_in)

    wq3 = w_uq.reshape(Q_LORA, MLA_HEADS, QK_NOPE + QK_ROPE)
    wq_rope = wq3[..., QK_NOPE:]
    wq_r = jnp.concatenate([wq3[..., :QK_NOPE], wq_rope, _swap_half(wq_rope)], axis=-1)
    wqt_r = wq_r.reshape(Q_LORA, MLA_HEADS * Q_HEAD_PAD).T.astype(BF16)
    wkv3 = w_ukv.reshape(KV_LORA, MLA_HEADS, QK_NOPE + V_HEAD)
    wk_r = wkv3[..., :QK_NOPE].reshape(KV_LORA, MLA_WIDTH).astype(BF16)
    wvt_r = wkv3[..., QK_NOPE:].reshape(KV_LORA, MLA_WIDTH).T.astype(BF16)

    u = _in_proj(x2, row(mix_pre_g), w_in_r)
    cosp, sinp, cost, sint = _rope_tables(seq)
    qt, kn, kr, vt = _mla_proj(u, cosp, sinp, cost, sint, row(q_norm_g), row(kv_norm_g), wqt_r, wk_r, wvt_r, seq)
    a_out = _mla_attn(qt, kn, kr, vt, batch, seq)
    b_out = _ssd(u, ssm_conv_w.astype(F32), row(ssm_conv_b), pad_lane(dt_bias), pad_lane(a_log),
                 row(jnp.repeat(d_skip, SSM_HEAD_DIM)), row(ssm_norm_g), batch, seq)
    w_out_b = w_out.astype(BF16)
    x1, h2 = _out_proj(a_out, b_out, x2, w_out_b[:MLA_WIDTH], w_out_b[MLA_WIDTH:],
                       row(mix_post_g), row(ffn_pre_g))
    return _ffn(h2, x1, w_gate.astype(BF16), w_up.astype(BF16), w_down.astype(BF16),
                jnp.concatenate([ffn_conv_w.astype(F32), row(ffn_conv_b)], axis=0), row(ffn_post_g), seq)


def kernel(x, mix_pre_g, w_in, q_norm_g, w_uq, kv_norm_g, w_ukv, ssm_conv_w, ssm_conv_b, dt_bias, a_log,
           d_skip, ssm_norm_g, w_out, mix_post_g, ffn_pre_g, w_gate, w_up, ffn_conv_w, ffn_conv_b,
           w_down, ffn_post_g):
    batch, seq, _ = x.shape
    x2 = x.reshape(batch * seq, D_MODEL)
    for l in range(mix_pre_g.shape[0]):
        x2 = _layer(x2, batch, seq, mix_pre_g[l], w_in[l], q_norm_g[l], w_uq[l], kv_norm_g[l], w_ukv[l],
                    ssm_conv_w[l], ssm_conv_b[l], dt_bias[l], a_log[l], d_skip[l], ssm_norm_g[l],
                    w_out[l], mix_post_g[l], ffn_pre_g[l], w_gate[l], w_up[l], ffn_conv_w[l],
                    ffn_conv_b[l], w_down[l], ffn_post_g[l])
    return x2.reshape(batch, seq, D_MODEL)
```

```python
import functools

import jax
import jax.numpy as jnp
import numpy as np
from jax import lax
from jax.experimental import pallas as pl
from jax.experimental.pallas import tpu as pltpu

F32 = jnp.float32
BF16 = jnp.bfloat16

D_MODEL = 2048
CHUNK = 64
EPS = 1e-6

MLA_HEADS = 8
Q_LORA = 768
KV_LORA = 512
QK_NOPE = 128
QK_ROPE = 64
V_HEAD = 128
ROPE_THETA = 10000.0
MLA_WIDTH = MLA_HEADS * V_HEAD
Q_HEAD_PAD = 256

SSM_HEADS = 16
SSM_HEAD_DIM = 64
SSM_INNER = SSM_HEADS * SSM_HEAD_DIM
SSM_GROUPS = 2
SSM_STATE = 128
SSM_CONV = 4
SSM_CONV_CH = SSM_INNER + 2 * SSM_GROUPS * SSM_STATE
GROUP_WIDTH = SSM_INNER // SSM_GROUPS
IN_COLS = Q_LORA + KV_LORA + QK_ROPE + SSM_INNER + SSM_CONV_CH + SSM_HEADS

D_FF = 5632
FFN_CONV = 3
FFN_ROWS = 256

LANE = 128
U_COLS = 4096
U_CQ = 0
U_KR = Q_LORA
U_DT = U_KR + LANE
U_CKV = U_DT + LANE
U_XBC = U_CKV + KV_LORA
U_Z = U_XBC + SSM_CONV_CH

VMEM_LIMIT = 56 * 1024 * 1024


def _rms(x, g):
    return x * lax.rsqrt(jnp.mean(x * x, axis=-1, keepdims=True) + EPS) * g


def _sigmoid(x):
    return 1.0 / (1.0 + jnp.exp(-x))


def _dot(a, b):
    return jnp.dot(a, b, preferred_element_type=F32)


def _dot_nt(a, b):
    return lax.dot_general(a, b, (((1,), (1,)), ((), ())), preferred_element_type=F32)


def _dot_tn(a, b):
    return lax.dot_general(a, b, (((0,), (0,)), ((), ())), preferred_element_type=F32)


def _split3(x, lane):
    xm = jnp.where(lane < SSM_HEADS, x, 0.0)
    hi = xm.astype(BF16).astype(F32)
    r1 = xm - hi
    mid = r1.astype(BF16).astype(F32)
    lo = (r1 - mid).astype(BF16).astype(F32)
    packed = hi + pltpu.roll(mid, SSM_HEADS, axis=1) + pltpu.roll(lo, 2 * SSM_HEADS, axis=1)
    return packed.astype(BF16)


def _in_proj_kernel(x_ref, g_ref, w_ref, o_ref):
    x = x_ref[...]
    r = lax.rsqrt(jnp.mean(x * x, axis=-1, keepdims=True) + EPS)
    o_ref[...] = _dot((x * g_ref[...]).astype(BF16), w_ref[...]) * r


def _in_proj(x2, g, w, tm=512):
    t = x2.shape[0]
    return pl.pallas_call(
        _in_proj_kernel,
        grid=(t // tm,),
        in_specs=[
            pl.BlockSpec((tm, D_MODEL), lambda i: (i, 0)),
            pl.BlockSpec((1, D_MODEL), lambda i: (0, 0)),
            pl.BlockSpec((D_MODEL, U_COLS), lambda i: (0, 0), pipeline_mode=pl.Buffered(1)),
        ],
        out_specs=pl.BlockSpec((tm, U_COLS), lambda i: (i, 0)),
        out_shape=jax.ShapeDtypeStruct((t, U_COLS), F32),
        compiler_params=pltpu.CompilerParams(
            dimension_semantics=("parallel",), vmem_limit_bytes=VMEM_LIMIT),
        name="in_proj",
    )(x2, g, w)


def _regroup_kernel(wt_ref, o_ref):
    src = np.cumsum([0, Q_LORA, KV_LORA, QK_ROPE, SSM_INNER, SSM_CONV_CH]).tolist()
    s_cq, s_ckv, s_kr, s_z, s_xbc, _ = src

    def piece(lo, width):
        return wt_ref[lo:lo + width, :].T

    for dst, lo, width in ((U_CQ, s_cq, Q_LORA), (U_CKV, s_ckv, KV_LORA), (U_XBC, s_xbc, SSM_CONV_CH),
                           (U_Z, s_z, SSM_INNER)):
        o_ref[:, dst:dst + width] = piece(lo, width).astype(BF16)
    kr = piece(s_kr, LANE)[:, :QK_ROPE]
    o_ref[:, U_KR:U_KR + QK_ROPE] = kr.astype(BF16)
    o_ref[:, U_KR + QK_ROPE:U_DT] = _swap_half(kr).astype(BF16)
    o_ref[:, U_DT:U_DT + SSM_HEADS] = piece(IN_COLS - LANE, LANE)[:, LANE - SSM_HEADS:].astype(BF16)
    o_ref[:, U_DT + SSM_HEADS:U_CKV] = jnp.zeros((o_ref.shape[0], LANE - SSM_HEADS), BF16)


def _regroup_w_in(w_in_t, tr=256):
    return pl.pallas_call(
        _regroup_kernel,
        grid=(D_MODEL // tr,),
        in_specs=[pl.BlockSpec((IN_COLS, tr), lambda i: (0, i))],
        out_specs=pl.BlockSpec((tr, U_COLS), lambda i: (i, 0)),
        out_shape=jax.ShapeDtypeStruct((D_MODEL, U_COLS), BF16),
        compiler_params=pltpu.CompilerParams(
            dimension_semantics=("parallel",), vmem_limit_bytes=VMEM_LIMIT),
        name="regroup_w_in",
    )(w_in_t)


def _rope_pair(blk, cosp, sinp):
    return blk * cosp + pltpu.roll(blk, QK_ROPE, axis=1) * sinp


def _mla_proj_kernel(cq_ref, ckv_ref, kr_ref, cos_ref, sin_ref, cost_ref, sint_ref, qg_ref, kvg_ref,
                     wqt_ref, wk_ref, wvt_ref, qt_ref, kn_ref, krot_ref, vt_ref, *, scale):
    cosp = cos_ref[...]
    sinp = sin_ref[...]
    cost = cost_ref[...]
    sint = sint_ref[...]
    qall = _dot_nt(wqt_ref[...], _rms(cq_ref[...], qg_ref[...]).astype(BF16))
    for h in range(MLA_HEADS):
        base = h * Q_HEAD_PAD
        rope = base + QK_NOPE
        qt_ref[base:rope, :] = (qall[base:rope, :] * scale).astype(BF16)
        rot = qall[rope:rope + QK_ROPE, :] * cost + qall[rope + QK_ROPE:base + Q_HEAD_PAD, :] * sint
        qt_ref[rope:rope + QK_ROPE, :] = (rot * scale).astype(BF16)
        qt_ref[rope + QK_ROPE:base + Q_HEAD_PAD, :] = jnp.zeros((QK_ROPE, rot.shape[1]), BF16)
    ckv = _rms(ckv_ref[...], kvg_ref[...]).astype(BF16)
    kn_ref[...] = _dot(ckv, wk_ref[...]).astype(BF16)
    vt_ref[...] = _dot_nt(wvt_ref[...], ckv).astype(BF16)
    krot_ref[...] = _rope_pair(kr_ref[...], cosp, sinp).astype(BF16)


def _mla_proj(u, cosp, sinp, cost, sint, qg, kvg, wqt, wk, wvt, seq, tm=512):
    t = u.shape[0]
    nseq = seq // tm
    scale = float((QK_NOPE + QK_ROPE) ** -0.5 * np.log2(np.e))
    row = lambda i: (i, 0)
    const = lambda i: (0, 0)
    return pl.pallas_call(
        functools.partial(_mla_proj_kernel, scale=scale),
        grid=(t // tm,),
        in_specs=[
            pl.BlockSpec((tm, Q_LORA), lambda i: (i, U_CQ // Q_LORA)),
            pl.BlockSpec((tm, KV_LORA), lambda i: (i, U_CKV // KV_LORA)),
            pl.BlockSpec((tm, LANE), lambda i: (i, U_KR // LANE)),
            pl.BlockSpec((tm, LANE), lambda i: (i % nseq, 0)),
            pl.BlockSpec((tm, LANE), lambda i: (i % nseq, 0)),
            pl.BlockSpec((QK_ROPE, tm), lambda i: (0, i % nseq)),
            pl.BlockSpec((QK_ROPE, tm), lambda i: (0, i % nseq)),
            pl.BlockSpec((1, Q_LORA), const),
            pl.BlockSpec((1, KV_LORA), const),
            pl.BlockSpec((MLA_HEADS * Q_HEAD_PAD, Q_LORA), const),
            pl.BlockSpec((KV_LORA, MLA_WIDTH), const),
            pl.BlockSpec((MLA_WIDTH, KV_LORA), const),
        ],
        out_specs=[
            pl.BlockSpec((MLA_HEADS * Q_HEAD_PAD, tm), lambda i: (0, i)),
            pl.BlockSpec((tm, MLA_WIDTH), row),
            pl.BlockSpec((tm, LANE), row),
            pl.BlockSpec((MLA_WIDTH, tm), lambda i: (0, i)),
        ],
        out_shape=[
            jax.ShapeDtypeStruct((MLA_HEADS * Q_HEAD_PAD, t), BF16),
            jax.ShapeDtypeStruct((t, MLA_WIDTH), BF16),
            jax.ShapeDtypeStruct((t, LANE), BF16),
            jax.ShapeDtypeStruct((MLA_WIDTH, t), BF16),
        ],
        compiler_params=pltpu.CompilerParams(
            dimension_semantics=("parallel",), vmem_limit_bytes=VMEM_LIMIT),
        name="mla_proj",
    )(u, u, u, cosp, sinp, cost, sint, qg, kvg, wqt, wk, wvt)


ONES_ROWS = 16


def _attn_kernel(qt_ref, kn_ref, kr_ref, vt_ref, o_ref, s_ref, mrun_ref, acc_ref, *, tq):
    qi = pl.program_id(1)
    sublanes = mrun_ref.shape[1]
    krow = lax.broadcasted_iota(jnp.int32, (tq, tq), 0) // CHUNK
    qcol = lax.broadcasted_iota(jnp.int32, (tq, tq), 1) // CHUNK
    diag_mask = krow <= qcol
    ones = jnp.ones((ONES_ROWS, tq), BF16)

    mrun_ref[...] = jnp.full(mrun_ref.shape, -1e30, F32)
    acc_ref[...] = jnp.zeros_like(acc_ref)

    def scores(j, masked):
        ks = pl.multiple_of(j * tq, tq)
        kr = kr_ref[pl.ds(ks, tq), :]
        for h in range(MLA_HEADS):
            qt = qt_ref[h * Q_HEAD_PAD:(h + 1) * Q_HEAD_PAD, :]
            k = jnp.concatenate([kn_ref[pl.ds(ks, tq), h * QK_NOPE:(h + 1) * QK_NOPE], kr], axis=-1)
            s = _dot(k, qt)
            if masked:
                s = jnp.where(diag_mask, s, -1e30)
            s_ref[h, j] = s
            tile_max = jnp.max(s.reshape(tq // sublanes, sublanes, tq), axis=0)
            mrun_ref[h] = jnp.maximum(mrun_ref[h], tile_max)

    def scores_pair(p, carry):
        scores(2 * p, masked=False)
        scores(2 * p + 1, masked=False)
        return carry

    lax.fori_loop(0, lax.shift_right_logical(qi, 1), scores_pair, 0)

    @pl.when(qi % 2 == 1)
    def _():
        scores(qi - 1, masked=False)

    scores(qi, masked=True)

    m = [jnp.max(mrun_ref[h], axis=0, keepdims=True) for h in range(MLA_HEADS)]

    def accumulate(j):
        ks = pl.multiple_of(j * tq, tq)
        for h in range(MLA_HEADS):
            p = jnp.exp2(s_ref[h, j] - m[h]).astype(BF16)
            vt = jnp.concatenate([vt_ref[h * V_HEAD:(h + 1) * V_HEAD, pl.ds(ks, tq)], ones], axis=0)
            acc_ref[h] += _dot(vt, p)

    def accumulate_pair(p, carry):
        accumulate(2 * p)
        accumulate(2 * p + 1)
        return carry

    lax.fori_loop(0, lax.shift_right_logical(qi + 1, 1), accumulate_pair, 0)

    @pl.when(qi % 2 == 0)
    def _():
        accumulate(qi)

    for h in range(MLA_HEADS):
        acc = acc_ref[h]
        out_t = acc[:V_HEAD, :] / acc[V_HEAD:V_HEAD + 1, :]
        o_ref[:, h * V_HEAD:(h + 1) * V_HEAD] = out_t.T.astype(BF16)


def _mla_attn(qt, kn, kr, vt, batch, seq, tq=256):
    t = kn.shape[0]
    nq = seq // tq
    return pl.pallas_call(
        functools.partial(_attn_kernel, tq=tq),
        grid=(batch, nq),
        in_specs=[
            pl.BlockSpec((MLA_HEADS * Q_HEAD_PAD, tq), lambda b, i: (0, b * nq + i)),
            pl.BlockSpec((seq, MLA_WIDTH), lambda b, i: (b, 0)),
            pl.BlockSpec((seq, LANE), lambda b, i: (b, 0)),
            pl.BlockSpec((MLA_WIDTH, seq), lambda b, i: (0, b)),
        ],
        out_specs=pl.BlockSpec((tq, MLA_WIDTH), lambda b, i: (b * nq + i, 0)),
        out_shape=jax.ShapeDtypeStruct((t, MLA_WIDTH), BF16),
        scratch_shapes=[
            pltpu.VMEM((MLA_HEADS, nq, tq, tq), F32),
            pltpu.VMEM((MLA_HEADS, 8, tq), F32),
            pltpu.VMEM((MLA_HEADS, V_HEAD + ONES_ROWS, tq), F32),
        ],
        compiler_params=pltpu.CompilerParams(
            dimension_semantics=("parallel", "arbitrary"), vmem_limit_bytes=VMEM_LIMIT),
        name="mla_attn",
    )(qt, kn, kr, vt)


def _ssd_kernel(z_ref, xbc_ref, prev_ref, dt_ref, cw_ref, cb_ref, dtb_ref, alog_ref, dexp_ref, ng_ref,
                o_ref, st_ref, xc_ref, dte_ref, acse_ref, wb_ref, *, ts):
    s_idx = pl.program_id(1)
    halo = prev_ref.shape[0]

    @pl.when(s_idx == 0)
    def _():
        st_ref[...] = jnp.zeros_like(st_ref)

    sub = lax.broadcasted_iota(jnp.int32, (halo, SSM_CONV_CH), 0)
    for k in range(SSM_CONV):
        wb_ref[k] = jnp.broadcast_to(cw_ref[k:k + 1, :], (halo, SSM_CONV_CH))
    wb_ref[SSM_CONV] = jnp.broadcast_to(cb_ref[...], (halo, SSM_CONV_CH))

    def conv_tile(i, prev):
        r = pl.multiple_of(i * halo, halo)
        cur = xbc_ref[pl.ds(r, halo), :]
        conv = wb_ref[SSM_CONV] + wb_ref[SSM_CONV - 1] * cur
        for d in range(1, SSM_CONV):
            shifted = jnp.where(sub < d, pltpu.roll(prev, d, axis=0), pltpu.roll(cur, d, axis=0))
            conv = conv + wb_ref[SSM_CONV - 1 - d] * shifted
        xc_ref[pl.ds(r, halo), :] = conv * _sigmoid(conv)
        return cur

    lax.fori_loop(0, ts // halo, conv_tile, jnp.where(s_idx > 0, prev_ref[...], 0.0), unroll=True)

    raw = dt_ref[...] + dtb_ref[...]
    dt = jnp.maximum(raw, 0.0) + jnp.log1p(jnp.exp(-jnp.abs(raw)))
    a = dt * (-jnp.exp(alog_ref[...]))
    lane = lax.broadcasted_iota(jnp.int32, (ts, LANE), 1)
    ri = lax.broadcasted_iota(jnp.int32, (ts, ts), 0)
    ci = lax.broadcasted_iota(jnp.int32, (ts, ts), 1)
    tri = jnp.where((ri // CHUNK == ci // CHUNK) & (ci <= ri), 1.0, 0.0).astype(BF16)
    c3 = _dot(tri, _split3(a, lane))
    acs = c3 + pltpu.roll(c3, LANE - SSM_HEADS, axis=1) + pltpu.roll(c3, LANE - 2 * SSM_HEADS, axis=1)
    er = lax.broadcasted_iota(jnp.int32, (LANE, SSM_INNER), 0)
    ec = lax.broadcasted_iota(jnp.int32, (LANE, SSM_INNER), 1) // SSM_HEAD_DIM
    expand = jnp.where((er % SSM_HEADS == ec) & (er < 3 * SSM_HEADS), 1.0, 0.0).astype(BF16)
    log2e = float(np.log2(np.e))
    both = _dot(jnp.concatenate([_split3(dt, lane), _split3(acs * log2e, lane)], axis=0), expand)
    dte_ref[...] = both[:ts]
    acse_ref[...] = both[ts:]

    li = lax.broadcasted_iota(jnp.int32, (CHUNK, SSM_INNER), 0)
    si = lax.broadcasted_iota(jnp.int32, (CHUNK, SSM_INNER), 1) % SSM_HEAD_DIM
    quad = 4 * SSM_HEAD_DIM
    bd_r = lax.broadcasted_iota(jnp.int32, (quad, quad), 0) // SSM_HEAD_DIM
    bd_c = lax.broadcasted_iota(jnp.int32, (quad, quad), 1) // SSM_HEAD_DIM
    bd_mask = bd_r == bd_c
    dexp = dexp_ref[...]
    ng = ng_ref[...]

    def chunk_body(c, carry):
        r0 = pl.multiple_of(c * CHUNK, CHUNK)
        rows = pl.ds(r0, CHUNK)
        xs = xc_ref[rows, 0:SSM_INNER]
        acx = acse_ref[rows, :]
        last = acx[CHUNK - 1:CHUNK, :]
        xdt = xs * dte_ref[rows, :]
        rv = jnp.sum(jnp.where(li == si, acx, 0.0), axis=0, keepdims=True)
        lmat = jnp.exp2(jnp.where(li >= si, acx - rv, -jnp.inf))
        xdec = (xdt * jnp.exp2(last - acx)).astype(BF16)
        xdt_b = xdt.astype(BF16)
        eacx = jnp.exp2(acx)
        elast = jnp.exp2(last)
        ys = []
        for g in range(SSM_GROUPS):
            gl = slice(g * GROUP_WIDTH, (g + 1) * GROUP_WIDTH)
            b_g = xc_ref[rows, SSM_INNER + g * SSM_STATE:SSM_INNER + (g + 1) * SSM_STATE].astype(BF16)
            c0 = SSM_INNER + SSM_GROUPS * SSM_STATE + g * SSM_STATE
            c_g = xc_ref[rows, c0:c0 + SSM_STATE].astype(BF16)
            gt = _dot_nt(c_g, jnp.concatenate([b_g] * 4, axis=0))
            st_g = st_ref[:, gl]
            y_off = _dot(c_g, st_g.astype(BF16)) * eacx[:, gl]
            yd = []
            for qq in range(GROUP_WIDTH // quad):
                sl = slice(g * GROUP_WIDTH + qq * quad, g * GROUP_WIDTH + (qq + 1) * quad)
                m_q = (gt * lmat[:, sl]).astype(BF16)
                x_q = jnp.concatenate([xdt_b[:, sl]] * 4, axis=0)
                x_q = jnp.where(bd_mask, x_q, jnp.zeros_like(x_q))
                yd.append(_dot(m_q, x_q))
            ys.append(jnp.concatenate(yd, axis=-1) + y_off)
            st_ref[:, gl] = st_g * elast[:, gl] + _dot_tn(b_g, xdec[:, gl])
        y = jnp.concatenate(ys, axis=-1) + xs * dexp
        z = z_ref[rows, :]
        y = y * (z * _sigmoid(z))
        outs = []
        for g in range(SSM_GROUPS):
            yg = y[:, g * GROUP_WIDTH:(g + 1) * GROUP_WIDTH]
            outs.append(yg * lax.rsqrt(jnp.mean(yg * yg, axis=-1, keepdims=True) + EPS))
        o_ref[rows, :] = (jnp.concatenate(outs, axis=-1) * ng).astype(BF16)
        return carry

    lax.fori_loop(0, ts // CHUNK, chunk_body, 0, unroll=True)


def _ssd(u, cw, cb, dtb, alog, dexp, ng, batch, seq, ts=512, halo=8):
    t = u.shape[0]
    ns = seq // ts
    const = lambda b, s: (0, 0)
    rowblk = lambda b, s: b * ns + s
    return pl.pallas_call(
        functools.partial(_ssd_kernel, ts=ts),
        grid=(batch, ns),
        in_specs=[
            pl.BlockSpec((ts, SSM_INNER), lambda b, s: (rowblk(b, s), U_Z // SSM_INNER)),
            pl.BlockSpec((ts, SSM_CONV_CH), lambda b, s: (rowblk(b, s), U_XBC // SSM_CONV_CH)),
            pl.BlockSpec((halo, SSM_CONV_CH),
                         lambda b, s: (jnp.maximum(rowblk(b, s) * (ts // halo) - 1, 0), U_XBC // SSM_CONV_CH)),
            pl.BlockSpec((ts, LANE), lambda b, s: (rowblk(b, s), U_DT // LANE)),
            pl.BlockSpec((SSM_CONV, SSM_CONV_CH), const),
            pl.BlockSpec((1, SSM_CONV_CH), const),
            pl.BlockSpec((1, LANE), const),
            pl.BlockSpec((1, LANE), const),
            pl.BlockSpec((1, SSM_INNER), const),
            pl.BlockSpec((1, SSM_INNER), const),
        ],
        out_specs=pl.BlockSpec((ts, SSM_INNER), lambda b, s: (rowblk(b, s), 0)),
        out_shape=jax.ShapeDtypeStruct((t, SSM_INNER), BF16),
        scratch_shapes=[
            pltpu.VMEM((SSM_STATE, SSM_INNER), F32),
            pltpu.VMEM((ts, SSM_CONV_CH), F32),
            pltpu.VMEM((ts, SSM_INNER), F32),
            pltpu.VMEM((ts, SSM_INNER), F32),
            pltpu.VMEM((SSM_CONV + 1, halo, SSM_CONV_CH), F32),
        ],
        compiler_params=pltpu.CompilerParams(
            dimension_semantics=("parallel", "arbitrary"), vmem_limit_bytes=VMEM_LIMIT),
        name="ssd",
    )(u, u, u, u, cw, cb, dtb, alog, dexp, ng)


def _out_proj_kernel(a_ref, b_ref, x_ref, wa_ref, wb_ref, g1_ref, g2_ref, x1_ref, h2_ref, *, piece):
    for r in range(0, a_ref.shape[0], piece):
        rows = slice(r, r + piece)
        mix = _dot(a_ref[rows, :], wa_ref[...]) + _dot(b_ref[rows, :], wb_ref[...])
        x1 = x_ref[rows, :] + _rms(mix, g1_ref[...])
        x1_ref[rows, :] = x1
        h2_ref[rows, :] = _rms(x1, g2_ref[...]).astype(BF16)


def _out_proj(a_out, b_out, x2, wa, wb, g1, g2, tm=512, piece=256):
    t = x2.shape[0]
    row = lambda i: (i, 0)
    const = lambda i: (0, 0)
    return pl.pallas_call(
        functools.partial(_out_proj_kernel, piece=piece),
        grid=(t // tm,),
        in_specs=[
            pl.BlockSpec((tm, MLA_WIDTH), row),
            pl.BlockSpec((tm, SSM_INNER), row),
            pl.BlockSpec((tm, D_MODEL), row),
            pl.BlockSpec((MLA_WIDTH, D_MODEL), const),
            pl.BlockSpec((SSM_INNER, D_MODEL), const),
            pl.BlockSpec((1, D_MODEL), const),
            pl.BlockSpec((1, D_MODEL), const),
        ],
        out_specs=[pl.BlockSpec((tm, D_MODEL), row), pl.BlockSpec((tm, D_MODEL), row)],
        out_shape=[jax.ShapeDtypeStruct((t, D_MODEL), F32), jax.ShapeDtypeStruct((t, D_MODEL), BF16)],
        compiler_params=pltpu.CompilerParams(
            dimension_semantics=("parallel",), vmem_limit_bytes=VMEM_LIMIT),
        name="out_proj",
    )(a_out, b_out, x2, wa, wb, g1, g2)


def _ffn_kernel(h_ref, x1_hbm, wg_ref, wu_ref, wd_ref, cw_ref, g_ref, o_ref,
                act_ref, tail_ref, x1_ref, x1_sem, *, tm, tf, seq):
    i = pl.program_id(0)
    j = pl.program_id(1)
    nt = pl.num_programs(0) - 1
    nf = tail_ref.shape[0]
    sub = tail_ref.shape[1]
    first = (i == 0) & (j == 0)
    cols = pl.ds(pl.multiple_of(j * tf, tf), tf)

    def x1_copy(tile):
        rows = pl.ds(pl.multiple_of(tile * tm, tm), tm)
        return pltpu.make_async_copy(x1_hbm.at[rows, :], x1_ref, x1_sem.at[0])

    @pl.when((i < nt) & (j == nf - 1))
    def _():
        x1_copy(i).start()

    def produce(slot):
        cw = cw_ref[:, cols]
        prev = tail_ref[j]
        row = lax.broadcasted_iota(jnp.int32, prev.shape, 0)
        c0 = float(np.sqrt(2.0 / np.pi))
        for r in range(0, tm, FFN_ROWS):
            rows = slice(r, r + FFN_ROWS)
            gate = _dot(h_ref[rows, :], wg_ref[...])
            up = _dot(h_ref[rows, :], wu_ref[...])
            pos = (i * tm + r) % seq + lax.broadcasted_iota(jnp.int32, (FFN_ROWS, 1), 0)
            conv = cw[FFN_CONV:FFN_CONV + 1, :] + cw[FFN_CONV - 1:FFN_CONV, :] * gate
            for k in range(FFN_CONV - 1):
                d = FFN_CONV - 1 - k
                rolled = pltpu.roll(gate, d, axis=0)
                head = jnp.where(row < d, pltpu.roll(prev, d, axis=0), rolled[:sub, :])
                shifted = jnp.concatenate([head, rolled[sub:, :]], axis=0)
                conv = conv + cw[k:k + 1, :] * jnp.where(pos >= d, shifted, 0.0)
            act = 0.5 * conv * (1.0 + jnp.tanh(c0 * (conv + 0.044715 * (conv * conv * conv))))
            act_ref[slot, rows, :] = (act * up).astype(BF16)
            prev = gate[FFN_ROWS - sub:, :]
        tail_ref[j] = prev

    def consume(slot):
        o_ref[...] += _dot(act_ref[slot], wd_ref[...])

    @pl.when(first)
    def _():
        tail_ref[...] = jnp.zeros_like(tail_ref)
        produce(0)

    @pl.when((i < nt) & (j == 1))
    def _():
        o_ref[...] = jnp.zeros_like(o_ref)

    for parity in range(2):
        @pl.when(jnp.logical_not(first) & (i < nt) & ((i + j) % 2 == parity))
        def _(parity=parity):
            produce(parity)
            consume(1 - parity)

    @pl.when((i == nt) & (j == 0))
    def _():
        consume((nt * nf - 1) % 2)

    @pl.when(jnp.logical_not(first) & (j == 0))
    def _():
        x1_copy(i - 1).wait()
        o_ref[...] = x1_ref[...] + _rms(o_ref[...], g_ref[...])


def _ffn(h2, x1, wg, wu, wd, cwb, g, seq, tm=1024, tf=512):
    t = h2.shape[0]
    nt = t // tm
    nf = D_FF // tf
    assert nf % 2 == 1
    row_p = lambda i, j: (jnp.minimum(i, nt - 1), 0)
    col_p = lambda i, j: (0, jnp.where(i < nt, j, nf - 1))
    row_c = lambda i, j: (jnp.where(j > 0, jnp.minimum(i, nt - 1), jnp.maximum(i - 1, 0)), 0)
    return pl.pallas_call(
        functools.partial(_ffn_kernel, tm=tm, tf=tf, seq=seq),
        grid=(nt + 1, nf),
        in_specs=[
            pl.BlockSpec((tm, D_MODEL), row_p),
            pl.BlockSpec(memory_space=pl.ANY),
            pl.BlockSpec((D_MODEL, tf), col_p),
            pl.BlockSpec((D_MODEL, tf), col_p),
            pl.BlockSpec((tf, D_MODEL), lambda i, j: (jnp.where(j > 0, j - 1, nf - 1), 0)),
            pl.BlockSpec((FFN_CONV + 1, D_FF), lambda i, j: (0, 0)),
            pl.BlockSpec((1, D_MODEL), lambda i, j: (0, 0)),
        ],
        out_specs=pl.BlockSpec((tm, D_MODEL), row_c),
        out_shape=jax.ShapeDtypeStruct((t, D_MODEL), F32),
        scratch_shapes=[
            pltpu.VMEM((2, tm, tf), BF16),
            pltpu.VMEM((nf, 8, tf), F32),
            pltpu.VMEM((tm, D_MODEL), F32),
            pltpu.SemaphoreType.DMA((1,)),
        ],
        compiler_params=pltpu.CompilerParams(
            dimension_semantics=("arbitrary", "arbitrary"), vmem_limit_bytes=VMEM_LIMIT),
        name="ffn",
    )(h2, x1, wg, wu, wd, cwb, g)


def _swap_half(w):
    half = w.shape[-1] // 2
    return jnp.concatenate([-w[..., half:], w[..., :half]], axis=-1)


def _rope_tables(seq):
    inv = 1.0 / (ROPE_THETA ** (jnp.arange(0, QK_ROPE, 2, dtype=F32) / QK_ROPE))
    ang = jnp.arange(seq, dtype=F32)[:, None] * inv[None, :]
    zeros = jnp.zeros((seq, LANE - QK_ROPE), F32)
    cos2 = jnp.concatenate([jnp.cos(ang), jnp.cos(ang)], axis=-1)
    sin2 = jnp.concatenate([jnp.sin(ang), jnp.sin(ang)], axis=-1)
    cosp = jnp.concatenate([cos2, zeros], axis=-1)
    sinp = jnp.concatenate([sin2, zeros], axis=-1)
    return cosp, sinp, cos2.T, sin2.T


def _layer(x2, batch, seq, mix_pre_g, w_in, q_norm_g, w_uq, kv_norm_g, w_ukv, ssm_conv_w, ssm_conv_b,
           dt_bias, a_log, d_skip, ssm_norm_g, w_out, mix_post_g, ffn_pre_g, w_gate, w_up,
           ffn_conv_w, ffn_conv_b, w_down, ffn_post_g):
    row = lambda v: v.reshape(1, -1).astype(F32)
    pad_lane = lambda v: jnp.pad(v.astype(F32), (0, LANE - v.shape[0])).reshape(1, LANE)

    w_in_r = _regroup_w_in(w_in.T)

    wq3 = w_uq.reshape(Q_LORA, MLA_HEADS, QK_NOPE + QK_ROPE)
    wq_rope = wq3[..., QK_NOPE:]
    wq_r = jnp.concatenate([wq3[..., :QK_NOPE], wq_rope, _swap_half(wq_rope)], axis=-1)
    wqt_r = wq_r.reshape(Q_LORA, MLA_HEADS * Q_HEAD_PAD).T.astype(BF16)
    wkv3 = w_ukv.reshape(KV_LORA, MLA_HEADS, QK_NOPE + V_HEAD)
    wk_r = wkv3[..., :QK_NOPE].reshape(KV_LORA, MLA_WIDTH).astype(BF16)
    wvt_r = wkv3[..., QK_NOPE:].reshape(KV_LORA, MLA_WIDTH).T.astype(BF16)

    u = _in_proj(x2, row(mix_pre_g), w_in_r)
    cosp, sinp, cost, sint = _rope_tables(seq)
    qt, kn, kr, vt = _mla_proj(u, cosp, sinp, cost, sint, row(q_norm_g), row(kv_norm_g), wqt_r, wk_r, wvt_r, seq)
    a_out = _mla_attn(qt, kn, kr, vt, batch, seq)
    b_out = _ssd(u, ssm_conv_w.astype(F32), row(ssm_conv_b), pad_lane(dt_bias), pad_lane(a_log),
                 row(jnp.repeat(d_skip, SSM_HEAD_DIM)), row(ssm_norm_g), batch, seq)
    w_out_b = w_out.astype(BF16)
    x1, h2 = _out_proj(a_out, b_out, x2, w_out_b[:MLA_WIDTH], w_out_b[MLA_WIDTH:],
                       row(mix_post_g), row(ffn_pre_g))
    return _ffn(h2, x1, w_gate.astype(BF16), w_up.astype(BF16), w_down.astype(BF16),
                jnp.concatenate([ffn_conv_w.astype(F32), row(ffn_conv_b)], axis=0), row(ffn_post_g), seq)


def kernel(x, mix_pre_g, w_in, q_norm_g, w_uq, kv_norm_g, w_ukv, ssm_conv_w, ssm_conv_b, dt_bias, a_log,
           d_skip, ssm_norm_g, w_out, mix_post_g, ffn_pre_g, w_gate, w_up, ffn_conv_w, ffn_conv_b,
           w_down, ffn_post_g):
    batch, seq, _ = x.shape
    x2 = x.reshape(batch * seq, D_MODEL)
    for l in range(mix_pre_g.shape[0]):
        x2 = _layer(x2, batch, seq, mix_pre_g[l], w_in[l], q_norm_g[l], w_uq[l], kv_norm_g[l], w_ukv[l],
                    ssm_conv_w[l], ssm_conv_b[l], dt_bias[l], a_log[l], d_skip[l], ssm_norm_g[l],
                    w_out[l], mix_post_g[l], ffn_pre_g[l], w_gate[l], w_up[l], ffn_conv_w[l],
                    ffn_conv_b[l], w_down[l], ffn_post_g[l])
    return x2.reshape(batch, seq, D_MODEL)
```
